```python
import math
import jax
import jax.numpy as jnp
from jax import lax
import numpy as np

D_MODEL = 2048
BATCH = 16
SEQ = 2048
DEPTH = 4
DEC_BATCH = 2
DEC_SEQ = 4096
PAST_LEN = 128

HEAD_DIM = 128
SSM_CH = 768
SSM_GROUP = 16
SSM_GROUPS = SSM_CH // SSM_GROUP
SSM_STATE = 64
SSM_DT_MIN = 0.001
SSM_DT_MAX = 0.1
SWA_Q_HEADS = 4
SWA_KV_HEADS = 2
SWA_HALF = 128
SWA_BLOCK = 128
DIL_PAIRS = ((128, 1), (512, 4), (2048, 16))
DIL_HEADS_PER_GROUP = 2
DIL_HEADS = DIL_HEADS_PER_GROUP * len(DIL_PAIRS)
ROPE_THETA = 500000.0
ROPE_DIM = HEAD_DIM // 4
D_FF = 7040
CONV_WIDTH = 3
NORM_EPS = 1e-6
NEG_INF = -1e30

A_W = SSM_CH
BQ_W = SWA_Q_HEADS * HEAD_DIM
BKV_W = SWA_KV_HEADS * HEAD_DIM
C_W = DIL_HEADS * HEAD_DIM
IN_COLS = A_W + BQ_W + 2 * BKV_W + 3 * C_W
OUT_ROWS = A_W + BQ_W + DIL_HEADS_PER_GROUP * HEAD_DIM
SPLITS = (A_W, A_W + BQ_W, A_W + BQ_W + BKV_W, A_W + BQ_W + 2 * BKV_W,
          A_W + BQ_W + 2 * BKV_W + C_W, A_W + BQ_W + 2 * BKV_W + 2 * C_W)

kernel_name = 'hybrid_s5_swa_dilated_encoder'


def _rmsnorm(x, g):
    xf = x.astype(jnp.float32)
    y = xf * lax.rsqrt(jnp.mean(xf * xf, axis=-1, keepdims=True) + NORM_EPS)
    return (y * g.astype(jnp.float32)).astype(x.dtype)


def _rope(x, pos):
    inv = ROPE_THETA ** (-jnp.arange(0, ROPE_DIM, 2, dtype=jnp.float32) / ROPE_DIM)
    ang = pos.astype(jnp.float32)[:, None] * inv[None, :]
    cos = jnp.cos(ang)[None, :, None, :]
    sin = jnp.sin(ang)[None, :, None, :]
    xf = x.astype(jnp.float32)
    half = ROPE_DIM // 2
    x1 = xf[..., :half]
    x2 = xf[..., half:ROPE_DIM]
    out = jnp.concatenate([x1 * cos - x2 * sin, x2 * cos + x1 * sin, xf[..., ROPE_DIM:]], axis=-1)
    return out.astype(x.dtype)


def _band_attention(q, k, v, kvalid, half, blk, sink=None):
    b, r, n, hk, g, dh = q.shape
    nb = n // blk
    pad = ((0, 0), (0, 0), (blk, blk), (0, 0), (0, 0))
    kp = jnp.pad(k, pad).reshape(b, r, nb + 2, blk, hk, dh)
    vp = jnp.pad(v, pad).reshape(b, r, nb + 2, blk, hk, dh)
    kw = jnp.concatenate([kp[:, :, :-2], kp[:, :, 1:-1], kp[:, :, 2:]], axis=3)
    vw = jnp.concatenate([vp[:, :, :-2], vp[:, :, 1:-1], vp[:, :, 2:]], axis=3)
    mp = jnp.pad(kvalid, ((0, 0), (blk, blk))).reshape(r, nb + 2, blk)
    mw = jnp.concatenate([mp[:, :-2], mp[:, 1:-1], mp[:, 2:]], axis=2)
    rel = jnp.arange(3 * blk)[None, :] - blk - jnp.arange(blk)[:, None]
    mask = (jnp.abs(rel) <= half)[None, None] & mw[:, :, None, :]
    qb = q.reshape(b, r, nb, blk, hk, g, dh)
    s = jnp.einsum('brnqhgd,brnkhd->brnhgqk', qb, kw).astype(jnp.float32) * (dh ** -0.5)
    s = jnp.where(mask[None, :, :, None, None], s, NEG_INF)
    m = jnp.max(s, axis=-1)
    if sink is not None:
        sink_b = sink.astype(jnp.float32).reshape(1, 1, 1, hk, g, 1)
        m = jnp.maximum(m, sink_b)
    p = jnp.exp(s - m[..., None])
    den = jnp.sum(p, axis=-1)
    if sink is not None:
        den = den + jnp.exp(sink_b - m)
    p = p / den[..., None]
    o = jnp.einsum('brnhgqk,brnkhd->brnqhgd', p.astype(v.dtype), vw,
                   preferred_element_type=jnp.float32)
    lse = jnp.moveaxis(m + jnp.log(den), -1, 3).reshape(b, r, n, hk, g)
    return o.reshape(b, r, n, hk, g, dh).astype(q.dtype), lse


def _cplx_combine(e1, e2):
    a1r, a1i, b1r, b1i = e1
    a2r, a2i, b2r, b2i = e2
    return (a2r * a1r - a2i * a1i,
            a2r * a1i + a2i * a1r,
            a2r * b1r - a2i * b1i + b2r,
            a2r * b1i + a2i * b1r + b2i)


def _s5(u, lam_re, lam_im, log_dt, b_re, b_im, c_re, c_im, d_skip, glu_w, glu_b):
    bsz, L, _ = u.shape
    uf = u.astype(jnp.float32).reshape(bsz, L, SSM_GROUPS, SSM_GROUP)
    y = d_skip.astype(jnp.float32).reshape(SSM_GROUPS, SSM_GROUP) * uf
    for r in range(2):
        dt = jnp.exp(log_dt[r].astype(jnp.float32))[:, None]
        lr = lam_re[r].astype(jnp.float32)
        li = lam_im[r].astype(jnp.float32)
        mag = jnp.exp(lr * dt)
        ab_re = mag * jnp.cos(li * dt)
        ab_im = mag * jnp.sin(li * dt)
        nr = ab_re - 1.0
        den = lr * lr + li * li
        z_re = (nr * lr + ab_im * li) / den
        z_im = (ab_im * lr - nr * li) / den
        br = b_re[r].astype(jnp.float32)
        bi = b_im[r].astype(jnp.float32)
        bb_re = z_re[..., None] * br - z_im[..., None] * bi
        bb_im = z_re[..., None] * bi + z_im[..., None] * br
        bu_re = jnp.einsum('blgc,gpc->blgp', uf, bb_re)
        bu_im = jnp.einsum('blgc,gpc->blgp', uf, bb_im)
        a_re = jnp.broadcast_to(ab_re, bu_re.shape)
        a_im = jnp.broadcast_to(ab_im, bu_im.shape)
        _, _, s_re, s_im = lax.associative_scan(_cplx_combine, (a_re, a_im, bu_re, bu_im),
                                                reverse=(r == 1), axis=1)
        y = y + jnp.einsum('blgp,gcp->blgc', s_re, c_re[r].astype(jnp.float32)) \
              - jnp.einsum('blgp,gcp->blgc', s_im, c_im[r].astype(jnp.float32))
    y = jax.nn.gelu(y.reshape(bsz, L, SSM_CH))
    out = y * jax.nn.sigmoid(y @ glu_w.astype(jnp.float32) + glu_b.astype(jnp.float32))
    return out.astype(u.dtype)


def _swa(q, k, v, sink):
    bsz, L = q.shape[0], q.shape[1]
    g = SWA_Q_HEADS // SWA_KV_HEADS
    qg = q.reshape(bsz, 1, L, SWA_KV_HEADS, g, HEAD_DIM)
    kvalid = jnp.ones((1, L), dtype=bool)
    o, _ = _band_attention(qg, k[:, None], v[:, None], kvalid, SWA_HALF, SWA_BLOCK,
                           sink.reshape(SWA_KV_HEADS, g))
    return o.reshape(bsz, L, BQ_W)


def _dilated(q, k, v):
    bsz, L = q.shape[0], q.shape[1]
    hp = DIL_HEADS_PER_GROUP
    outs, lses = [], []
    for gi, (w, d) in enumerate(DIL_PAIRS):
        side = w // (2 * d)
        blk = side
        span = blk * d
        lp = -(-L // span) * span
        n = lp // d

        def fold(t):
            t = jnp.pad(t[:, :, gi * hp:(gi + 1) * hp], ((0, 0), (0, lp - L), (0, 0), (0, 0)))
            return t.reshape(bsz, n, d, hp, HEAD_DIM).transpose(0, 2, 1, 3, 4)

        kvalid = (jnp.arange(lp) < L).reshape(n, d).T
        o, lse = _band_attention(fold(q)[:, :, :, :, None, :], fold(k), fold(v), kvalid, side, blk)
        outs.append(o[:, :, :, :, 0].transpose(0, 2, 1, 3, 4).reshape(bsz, lp, hp, HEAD_DIM)[:, :L])
        lses.append(lse[..., 0].transpose(0, 2, 1, 3).reshape(bsz, lp, hp)[:, :L])
    alpha = jax.nn.softmax(jnp.stack(lses), axis=0)
    o = jnp.einsum('gblh,gblhd->blhd', alpha, jnp.stack(outs).astype(jnp.float32))
    return o.reshape(bsz, L, hp * HEAD_DIM).astype(q.dtype)


def _ffn(x, w_gate, w_up, conv_w, conv_b, w_down):
    g = x @ w_gate
    g = lax.conv_general_dilated(g, conv_w[:, None, :], window_strides=(1,),
                                 padding=((CONV_WIDTH // 2, CONV_WIDTH // 2),),
                                 dimension_numbers=('NWC', 'WIO', 'NWC'),
                                 feature_group_count=D_FF) + conv_b
    h = jax.nn.gelu(g, approximate=True) * (x @ w_up)
    return h @ w_down


def _layer(x, ln_mix_pre, ln_mix_post, w_in, lam_re, lam_im, log_dt, b_re, b_im, c_re, c_im,
           d_skip, glu_w, glu_b, sink, w_out, ln_ffn_pre, ln_ffn_post, w_gate, w_up, conv_w,
           conv_b, w_down):
    bsz, L, _ = x.shape
    h = _rmsnorm(x, ln_mix_pre)
    z = h @ w_in
    u, bq, bk, bv, cq, ck, cv = jnp.split(z, SPLITS, axis=-1)
    pos = jnp.arange(L)
    heads = lambda t: t.reshape(bsz, L, -1, HEAD_DIM)
    y_a = _s5(u, lam_re, lam_im, log_dt, b_re, b_im, c_re, c_im, d_skip, glu_w, glu_b)
    y_b = _swa(_rope(heads(bq), pos), _rope(heads(bk), pos), heads(bv), sink)
    y_c = _dilated(_rope(heads(cq), pos), _rope(heads(ck), pos), heads(cv))
    mix = jnp.concatenate([y_a, y_b, y_c], axis=-1) @ w_out
    x = x + _rmsnorm(mix, ln_mix_post)
    f = _ffn(_rmsnorm(x, ln_ffn_pre), w_gate, w_up, conv_w, conv_b, w_down)
    return x + _rmsnorm(f, ln_ffn_post)


def _trunk(x, params):
    (ln_mix_pre, ln_mix_post, w_in, ssm_lam_re, ssm_lam_im, ssm_log_dt, ssm_b_re, ssm_b_im,
     ssm_c_re, ssm_c_im, ssm_d, ssm_glu_w, ssm_glu_b, swa_sink, w_out, ln_ffn_pre, ln_ffn_post,
     ffn_w_gate, ffn_w_up, ffn_conv_w, ffn_conv_b, ffn_w_down) = params
    for l in range(DEPTH):
        x = _layer(x, ln_mix_pre[l], ln_mix_post[l], w_in[l], ssm_lam_re[l], ssm_lam_im[l],
                   ssm_log_dt[l], ssm_b_re[l], ssm_b_im[l], ssm_c_re[l], ssm_c_im[l], ssm_d[l],
                   ssm_glu_w[l], ssm_glu_b[l], swa_sink[l], w_out[l], ln_ffn_pre[l], ln_ffn_post[l],
                   ffn_w_gate[l], ffn_w_up[l], ffn_conv_w[l], ffn_conv_b[l], ffn_w_down[l])
    return x


def setup_inputs(seed: int = 0) -> dict:
    key = jax.random.key(seed)
    ks = jax.random.split(key, 24)
    f32 = jnp.float32

    def nrm(k, shape, scale):
        return scale * jax.random.normal(k, shape, f32)

    G, P, C = SSM_GROUPS, SSM_STATE, SSM_GROUP
    lam_im0 = math.pi * jnp.arange(P, dtype=f32)
    return {
        'x_prompt': nrm(ks[0], (BATCH, SEQ, D_MODEL), 1.0),
        'x_sample': nrm(ks[1], (DEC_BATCH, DEC_SEQ, D_MODEL), 1.0),
        'ln_mix_pre': 1.0 + nrm(ks[2], (DEPTH, D_MODEL), 0.05),
        'ln_mix_post': 1.0 + nrm(ks[3], (DEPTH, D_MODEL), 0.05),
        'w_in': nrm(ks[4], (DEPTH, D_MODEL, IN_COLS), D_MODEL ** -0.5),
        'ssm_lam_re': -0.5 + nrm(ks[5], (DEPTH, 2, G, P), 0.01),
        'ssm_lam_im': lam_im0 + nrm(ks[6], (DEPTH, 2, G, P), 0.01),
        'ssm_log_dt': jax.random.uniform(ks[7], (DEPTH, 2, G), f32,
                                         math.log(SSM_DT_MIN), math.log(SSM_DT_MAX)),
        'ssm_b_re': nrm(ks[8], (DEPTH, 2, G, P, C), (2 * C) ** -0.5),
        'ssm_b_im': nrm(ks[9], (DEPTH, 2, G, P, C), (2 * C) ** -0.5),
        'ssm_c_re': nrm(ks[10], (DEPTH, 2, G, C, P), P ** -0.5),
        'ssm_c_im': nrm(ks[11], (DEPTH, 2, G, C, P), P ** -0.5),
        'ssm_d': nrm(ks[12], (DEPTH, SSM_CH), 1.0),
        'ssm_glu_w': nrm(ks[13], (DEPTH, SSM_CH, SSM_CH), SSM_CH ** -0.5),
        'ssm_glu_b': nrm(ks[14], (DEPTH, SSM_CH), 0.01),
        'swa_sink': nrm(ks[15], (DEPTH, SWA_Q_HEADS), 0.5),
        'w_out': nrm(ks[16], (DEPTH, OUT_ROWS, D_MODEL), OUT_ROWS ** -0.5),
        'ln_ffn_pre': 1.0 + nrm(ks[17], (DEPTH, D_MODEL), 0.05),
        'ln_ffn_post': 1.0 + nrm(ks[18], (DEPTH, D_MODEL), 0.05),
        'ffn_w_gate': nrm(ks[19], (DEPTH, D_MODEL, D_FF), D_MODEL ** -0.5),
        'ffn_w_up': nrm(ks[20], (DEPTH, D_MODEL, D_FF), D_MODEL ** -0.5),
        'ffn_conv_w': nrm(ks[21], (DEPTH, CONV_WIDTH, D_FF), CONV_WIDTH ** -0.5),
        'ffn_conv_b': nrm(ks[22], (DEPTH, D_FF), 0.01),
        'ffn_w_down': nrm(ks[23], (DEPTH, D_FF, D_MODEL), D_FF ** -0.5),
    }


def reference(x_prompt, x_sample, ln_mix_pre, ln_mix_post, w_in, ssm_lam_re, ssm_lam_im, ssm_log_dt,
              ssm_b_re, ssm_b_im, ssm_c_re, ssm_c_im, ssm_d, ssm_glu_w, ssm_glu_b, swa_sink, w_out,
              ln_ffn_pre, ln_ffn_post, ffn_w_gate, ffn_w_up, ffn_conv_w, ffn_conv_b, ffn_w_down):
    params = (ln_mix_pre, ln_mix_post, w_in, ssm_lam_re, ssm_lam_im, ssm_log_dt, ssm_b_re, ssm_b_im,
              ssm_c_re, ssm_c_im, ssm_d, ssm_glu_w, ssm_glu_b, swa_sink, w_out, ln_ffn_pre,
              ln_ffn_post, ffn_w_gate, ffn_w_up, ffn_conv_w, ffn_conv_b, ffn_w_down)
    y_prompt = _trunk(x_prompt, params)
    y_sample = _trunk(x_sample, params)
    return (y_prompt, y_sample)
```

```python
import functools
import math

import jax
import jax.numpy as jnp
from jax import lax
from jax.experimental import pallas as pl
from jax.experimental.pallas import tpu as pltpu

F32 = jnp.float32
BF16 = jnp.bfloat16

D_MODEL = 2048
HEAD_DIM = 128
SSM_CH = 768
SSM_GROUP = 16
SSM_GROUPS = SSM_CH // SSM_GROUP
SSM_PAIRS = SSM_GROUPS // 2
SSM_STATE = 64
SWA_Q_HEADS = 4
SWA_KV_HEADS = 2
SWA_HALF = 128
DIL_PAIRS = ((128, 1), (512, 4), (2048, 16))
DIL_HEADS_PER_GROUP = 2
DIL_HEADS = DIL_HEADS_PER_GROUP * len(DIL_PAIRS)
ROPE_THETA = 500000.0
ROPE_DIM = HEAD_DIM // 4
D_FF = 7040
NORM_EPS = 1e-6
NEG_INF = -1e30

A_W = SSM_CH
BQ_W = SWA_Q_HEADS * HEAD_DIM
BKV_W = SWA_KV_HEADS * HEAD_DIM
C_W = DIL_HEADS * HEAD_DIM
IN_COLS = A_W + BQ_W + 2 * BKV_W + 3 * C_W
OUT_ROWS = A_W + BQ_W + DIL_HEADS_PER_GROUP * HEAD_DIM
BQ_H0 = A_W // HEAD_DIM
BK_H0 = BQ_H0 + SWA_Q_HEADS
BV_H0 = BK_H0 + SWA_KV_HEADS
CQ_H0 = BV_H0 + SWA_KV_HEADS
CK_H0 = CQ_H0 + DIL_HEADS
CV_H0 = CK_H0 + DIL_HEADS
IN_HEADS = IN_COLS // HEAD_DIM

LANES = 128
SUBLANES = 8
BF16_ROWS = 16
MXU_DIM = 256
VMEM_LIMIT = 56 << 20

S5_CHUNK = 16
S5_PAIR_W = 2 * S5_CHUNK * SSM_GROUP
D_FF_PAD = -(-D_FF // (2 * MXU_DIM)) * (2 * MXU_DIM)


def _params(sem):
    return pltpu.CompilerParams(dimension_semantics=sem, vmem_limit_bytes=VMEM_LIMIT)


def _rms(x, g):
    ms = jnp.mean(x * x, axis=-1, keepdims=True)
    return x * lax.rsqrt(ms + NORM_EPS) * g


def _gelu_tanh(x):
    c = math.sqrt(2.0 / math.pi)
    return 0.5 * x * (1.0 + jnp.tanh(c * (x + 0.044715 * (x * x * x))))


def _is_rope_head(h):
    return ((h >= BQ_H0) & (h < BV_H0)) | ((h >= CQ_H0) & (h < CV_H0))


def _inproj_kernel(x_ref, g_ref, w_ref, cos_ref, sa_ref, sb_ref, o_ref, xn_ref, *, heads_per_tile):
    j = pl.program_id(1)

    @pl.when(j == 0)
    def _():
        xn_ref[...] = _rms(x_ref[...], g_ref[...]).astype(BF16)

    acc = jnp.dot(xn_ref[...], w_ref[...], preferred_element_type=F32)
    for h in range(heads_per_tile):
        cols = slice(h * HEAD_DIM, (h + 1) * HEAD_DIM)
        zh = acc[:, cols]
        rope = _is_rope_head(j * heads_per_tile + h)

        @pl.when(rope)
        def _():
            r = (zh * cos_ref[...] + pltpu.roll(zh, HEAD_DIM - ROPE_DIM // 2, 1) * sa_ref[...]
                 + pltpu.roll(zh, ROPE_DIM // 2, 1) * sb_ref[...])
            o_ref[:, cols] = r.astype(BF16)

        @pl.when(jnp.logical_not(rope))
        def _():
            o_ref[:, cols] = zh.astype(BF16)


def _inproj(x, g, w, tabs, seq_len, *, tm=512, tn=512):
    t = x.shape[0]
    cos_t, sa_t, sb_t = tabs
    tiles_per_seq = seq_len // tm
    tab_spec = pl.BlockSpec((tm, HEAD_DIM), lambda i, j: (i % tiles_per_seq, 0))
    return pl.pallas_call(
        functools.partial(_inproj_kernel, heads_per_tile=tn // HEAD_DIM),
        grid=(t // tm, IN_COLS // tn),
        in_specs=[
            pl.BlockSpec((tm, D_MODEL), lambda i, j: (i, 0)),
            pl.BlockSpec((1, D_MODEL), lambda i, j: (0, 0)),
            pl.BlockSpec((D_MODEL, tn), lambda i, j: (0, j)),
            tab_spec, tab_spec, tab_spec,
        ],
        out_specs=pl.BlockSpec((tm, tn), lambda i, j: (i, j)),
        out_shape=jax.ShapeDtypeStruct((t, IN_COLS), BF16),
        scratch_shapes=[pltpu.VMEM((tm, D_MODEL), BF16)],
        compiler_params=_params(("parallel", "arbitrary")),
        name="inproj",
    )(x, g, w, cos_t, sa_t, sb_t)


def _rope_tables(seq_len):
    half = ROPE_DIM // 2
    inv = ROPE_THETA ** (-jnp.arange(0, ROPE_DIM, 2, dtype=F32) / ROPE_DIM)
    ang = jnp.arange(seq_len, dtype=F32)[:, None] * inv[None, :]
    cos, sin = jnp.cos(ang), jnp.sin(ang)
    zeros = jnp.zeros((seq_len, HEAD_DIM - ROPE_DIM), F32)
    zh = jnp.zeros((seq_len, half), F32)
    cos_t = jnp.concatenate([cos, cos, jnp.ones((seq_len, HEAD_DIM - ROPE_DIM), F32)], axis=1)
    sa_t = jnp.concatenate([-sin, zh, zeros], axis=1)
    sb_t = jnp.concatenate([zh, sin, zeros], axis=1)
    return cos_t, sa_t, sb_t


def _s5prep_kernel(lr_ref, li_ref, ldt_ref, bre_ref, bim_ref, cre_ref, cim_ref, d_ref,
                   wt_ref, ct_ref, aq_ref, kt_ref):
    q = S5_CHUNK
    p = SSM_STATE
    kts = []
    for r in range(2):
        lr = lr_ref[r]
        li = li_ref[r]
        dt = jnp.exp(ldt_ref[r])
        mag = jnp.exp(lr * dt)
        ab_re = mag * jnp.cos(li * dt)
        ab_im = mag * jnp.sin(li * dt)
        nr = ab_re - 1.0
        den = lr * lr + li * li
        z_re = (nr * lr + ab_im * li) / den
        z_im = (ab_im * lr - nr * li) / den
        b_re = bre_ref[r]
        b_im = bim_ref[r]
        bb_re = z_re * b_re - z_im * b_im
        bb_im = z_re * b_im + z_im * b_re
        c_re = cre_ref[r]
        c_im = cim_ref[r]
        pw_re = jnp.ones((1, p), F32)
        pw_im = jnp.zeros((1, p), F32)
        for k in range(q + 1):
            if k < q:
                wt_ref[r, k, :, 0:p] = pw_re * bb_re - pw_im * bb_im
                wt_ref[r, k, :, p:2 * p] = pw_re * bb_im + pw_im * bb_re
            ct_ref[r, k, :, 0:p] = c_re * pw_re - c_im * pw_im
            ct_ref[r, k, :, p:2 * p] = -(c_re * pw_im + c_im * pw_re)
            if k < q:
                pw_re, pw_im = pw_re * ab_re - pw_im * ab_im, pw_re * ab_im + pw_im * ab_re
        aq_ref[r, :, 0:p] = pw_re
        aq_ref[r, :, p:2 * p] = pw_im
        wt_flat = wt_ref[r].reshape(q * SSM_GROUP, 2 * p)
        kts.append(lax.dot_general(wt_flat, ct_ref[r, 0], (((1,), (1,)), ((), ())),
                                   precision=lax.Precision.HIGHEST,
                                   preferred_element_type=F32))
    ktf, ktb = kts
    rows = lax.broadcasted_iota(jnp.int32, (SSM_GROUP, SSM_GROUP), 0)
    cols = lax.broadcasted_iota(jnp.int32, (SSM_GROUP, SSM_GROUP), 1)
    diag = jnp.where(rows == cols, jnp.broadcast_to(d_ref[...], (SSM_GROUP, SSM_GROUP)), 0.0)
    kt_ref[q - 1] = ktf[0:SSM_GROUP] + ktb[0:SSM_GROUP] + diag
    for k in range(1, q):
        blk = slice(k * SSM_GROUP, (k + 1) * SSM_GROUP)
        kt_ref[q - 1 + k] = ktf[blk]
        kt_ref[q - 1 - k] = ktb[blk]


def _s5_prep(lam_re, lam_im, log_dt, b_re, b_im, c_re, c_im, d_skip):
    g, p, c, q = SSM_GROUPS, SSM_STATE, SSM_GROUP, S5_CHUNK
    lr = lam_re.reshape(2, g, 1, p)
    li = lam_im.reshape(2, g, 1, p)
    ldt = jnp.broadcast_to(log_dt.reshape(2, g, 1, 1), (2, g, 1, p))
    bt_re = jnp.swapaxes(b_re, -1, -2)
    bt_im = jnp.swapaxes(b_im, -1, -2)
    d3 = d_skip.reshape(g, 1, c)
    vec_spec = pl.BlockSpec((2, None, 1, p), lambda i: (0, i, 0, 0))
    mat_spec = pl.BlockSpec((2, None, c, p), lambda i: (0, i, 0, 0))
    wt, ct, aq, kt = pl.pallas_call(
        _s5prep_kernel,
        grid=(g,),
        in_specs=[vec_spec, vec_spec, vec_spec, mat_spec, mat_spec, mat_spec, mat_spec,
                  pl.BlockSpec((None, 1, c), lambda i: (i, 0, 0))],
        out_specs=[
            pl.BlockSpec((None, 2, q, c, 2 * p), lambda i: (i, 0, 0, 0, 0)),
            pl.BlockSpec((None, 2, q + 1, c, 2 * p), lambda i: (i, 0, 0, 0, 0)),
            pl.BlockSpec((None, 2, 1, 2 * p), lambda i: (i, 0, 0, 0)),
            pl.BlockSpec((None, 2 * q - 1, c, c), lambda i: (i, 0, 0, 0)),
        ],
        out_shape=[
            jax.ShapeDtypeStruct((g, 2, q, c, 2 * p), F32),
            jax.ShapeDtypeStruct((g, 2, q + 1, c, 2 * p), F32),
            jax.ShapeDtypeStruct((g, 2, 1, 2 * p), F32),
            jax.ShapeDtypeStruct((g, 2 * q - 1, c, c), F32),
        ],
        compiler_params=_params(("parallel",)),
        name="s5prep",
    )(lr, li, ldt, bt_re, bt_im, c_re, c_im, d3)

    gp = SSM_PAIRS
    qc = q * c
    idx = jnp.arange(q)[None, :] - jnp.arange(q)[:, None] + (q - 1)
    toep = kt[:, idx].transpose(0, 1, 3, 2, 4).reshape(gp, 2, qc, qc).astype(BF16)

    bf = wt[:, 0, ::-1].reshape(g, qc, 2 * p)
    bb = wt[:, 1].reshape(g, qc, 2 * p)
    xg = jnp.stack([bf[..., :p], bf[..., p:], bb[..., :p], bb[..., p:]], axis=2)
    xg = xg.reshape(gp, 2, qc, 4, p)
    zx = jnp.zeros_like(xg[:, 0])
    w1 = jnp.concatenate([
        jnp.stack([xg[:, 0], zx], axis=-2).reshape(gp, qc, 8 * p),
        jnp.stack([zx, xg[:, 1]], axis=-2).reshape(gp, qc, 8 * p)], axis=1).astype(BF16)

    cf = ct[:, 0, 1:]
    cb = ct[:, 1, 1:][:, ::-1]
    yg = jnp.stack([cf[..., :p], cf[..., p:], cb[..., :p], cb[..., p:]], axis=1)
    yg = yg.transpose(0, 1, 4, 2, 3).reshape(gp, 2, 4, p, qc)
    zy = jnp.zeros_like(yg[:, 0])
    w2 = jnp.stack([jnp.stack([yg[:, 0], zy], axis=-2),
                    jnp.stack([zy, yg[:, 1]], axis=-2)], axis=2)
    w2 = w2.reshape(gp, 8 * p, 2 * qc).astype(BF16)

    aqp = aq.reshape(gp, 2, 2, 2, p).transpose(0, 2, 3, 1, 4).reshape(gp, 2, 2, 2 * p)
    return toep, w1, w2, aqp


def _s5_kernel(u_ref, w1_ref, t_ref, w2_ref, aq_ref, o_ref, e_ref, *, pairs, nc, bp, rb):
    m = nc * bp
    qc = S5_CHUNK * SSM_GROUP
    for p in range(pairs):
        for r0 in range(0, m, rb):
            rows = slice(r0, r0 + rb)
            e_ref[p, rows, :] = jnp.dot(u_ref[p, rows, :], w1_ref[p], preferred_element_type=F32)

    coef = [[[jnp.broadcast_to(aq_ref[p, d, ri:ri + 1, :], (bp, LANES)) for ri in range(2)]
             for d in range(2)] for p in range(pairs)]

    def body(i, carry):
        out = []
        for p in range(pairs):
            sfr, sfi, sbr, sbi = carry[p]
            (afr, afi), (abr, abi) = coef[p]
            rf = pl.ds(pl.multiple_of(i * bp, bp), bp)
            rv = pl.ds(pl.multiple_of((nc - 1 - i) * bp, bp), bp)
            efr = e_ref[p, rf, 0:LANES]
            efi = e_ref[p, rf, LANES:2 * LANES]
            ebr = e_ref[p, rv, 2 * LANES:3 * LANES]
            ebi = e_ref[p, rv, 3 * LANES:4 * LANES]
            e_ref[p, rf, 0:LANES] = sfr
            e_ref[p, rf, LANES:2 * LANES] = sfi
            e_ref[p, rv, 2 * LANES:3 * LANES] = sbr
            e_ref[p, rv, 3 * LANES:4 * LANES] = sbi
            out.append((afr * sfr - afi * sfi + efr, afr * sfi + afi * sfr + efi,
                        abr * sbr - abi * sbi + ebr, abr * sbi + abi * sbr + ebi))
        return tuple(out)

    zero = jnp.zeros((bp, LANES), F32)
    lax.fori_loop(0, nc, body, tuple((zero, zero, zero, zero) for _ in range(pairs)))

    for p in range(pairs):
        for r0 in range(0, m, rb):
            rows = slice(r0, r0 + rb)
            y = jnp.dot(e_ref[p, rows, :].astype(BF16), w2_ref[p], preferred_element_type=F32)
            for e in range(2):
                cols = slice(e * qc, (e + 1) * qc)
                ye = y[:, cols] + jnp.dot(u_ref[p, rows, cols], t_ref[p, e],
                                          preferred_element_type=F32)
                o_ref[p, rows, cols] = _gelu_tanh(ye).astype(BF16)


def _s5_mix(up, toep, w1, w2, aqp, *, nc, bp, pairs=2, rb=512):
    gp, m, w = up.shape
    qc = S5_CHUNK * SSM_GROUP
    return pl.pallas_call(
        functools.partial(_s5_kernel, pairs=pairs, nc=nc, bp=bp, rb=rb),
        grid=(gp // pairs,),
        in_specs=[
            pl.BlockSpec((pairs, m, w), lambda i: (i, 0, 0)),
            pl.BlockSpec((pairs, w, w), lambda i: (i, 0, 0)),
            pl.BlockSpec((pairs, 2, qc, qc), lambda i: (i, 0, 0, 0)),
            pl.BlockSpec((pairs, w, w), lambda i: (i, 0, 0)),
            pl.BlockSpec((pairs, 2, 2, LANES), lambda i: (i, 0, 0, 0)),
        ],
        out_specs=pl.BlockSpec((pairs, m, w), lambda i: (i, 0, 0)),
        out_shape=jax.ShapeDtypeStruct((gp, m, w), BF16),
        scratch_shapes=[pltpu.VMEM((pairs, m, w), F32)],
        compiler_params=_params(("parallel",)),
        name="s5mix",
    )(up, w1, toep, w2, aqp)


def _s5_branch(z, bsz, seq_len, ops):
    toep, w1, w2, aqp = ops
    q, c, gp = S5_CHUNK, SSM_GROUP, SSM_PAIRS
    nc = seq_len // q
    bp = -(-bsz // SUBLANES) * SUBLANES
    u = z[:, :SSM_CH].reshape(bsz, nc, q, gp, 2, c)
    u = u.transpose(3, 1, 0, 4, 2, 5)
    u = jnp.pad(u, ((0, 0), (0, 0), (0, bp - bsz), (0, 0), (0, 0), (0, 0)))
    up = u.reshape(gp, nc * bp, S5_PAIR_W)
    y = _s5_mix(up, toep, w1, w2, aqp, nc=nc, bp=bp)
    y = y.reshape(gp, nc, bp, 2, q, c)[:, :, :bsz]
    return y.transpose(2, 1, 4, 0, 3, 5).reshape(bsz * seq_len, SSM_CH)


def _band_kernel(*refs, n, tq, half, heads, has_sink, want_lse):
    if has_sink:
        sink_ref, refs = refs[0], refs[1:]
    q_ref, k_ref, v_ref, o_ref = refs[:4]
    lse_ref = refs[4] if want_lse else None
    w = min(tq + 2 * half, n)
    scale = HEAD_DIM ** -0.5
    sinks = [sink_ref[pl.program_id(1) * heads + g] for g in range(heads)] if has_sink else None
    col =lax.broadcasted_iota(jnp.int32, (tq, w), 1)
    row = lax.broadcasted_iota(jnp.int32, (tq, w), 0)
    align = math.gcd(math.gcd(tq, half), n - w) if n > w else tq

    def body(i, _):
        q0 = pl.multiple_of(i * tq, tq)
        start = pl.multiple_of(jnp.clip(q0 - half, 0, n - w), align)
        k = k_ref[pl.ds(start, w), :]
        v = v_ref[pl.ds(start, w), :]
        mask = jnp.abs(col - row + (start - q0)) <= half
        for g in range(heads):
            cols = slice(g * HEAD_DIM, (g + 1) * HEAD_DIM)
            q = q_ref[pl.ds(q0, tq), cols]
            s = lax.dot_general(q, k, (((1,), (1,)), ((), ())), preferred_element_type=F32) * scale
            s = jnp.where(mask, s, NEG_INF)
            mx = jnp.max(s, axis=-1, keepdims=True)
            if has_sink:
                sk = sinks[g]
                mx = jnp.maximum(mx, sk)
            p = jnp.exp(s - mx)
            den = jnp.sum(p, axis=-1, keepdims=True)
            if has_sink:
                den = den + jnp.exp(sk - mx)
            o = jnp.dot(p.astype(BF16), v, preferred_element_type=F32) / den
            o_ref[pl.ds(q0, tq), cols] = o.astype(o_ref.dtype)
            if want_lse:
                lse_ref[pl.ds(q0, tq), cols] = jnp.broadcast_to(mx + jnp.log(den), (tq, HEAD_DIM))
        return 0

    lax.fori_loop(0, n // tq, body, 0)


def _swa(z3, sink, *, tq=128):
    bsz, n, _ = z3.shape
    g = SWA_Q_HEADS // SWA_KV_HEADS
    return pl.pallas_call(
        functools.partial(_band_kernel, n=n, tq=tq, half=SWA_HALF, heads=g, has_sink=True,
                          want_lse=False),
        grid=(bsz, SWA_KV_HEADS),
        in_specs=[
            pl.BlockSpec(memory_space=pltpu.SMEM),
            pl.BlockSpec((None, n, g * HEAD_DIM), lambda b, h: (b, 0, BQ_H0 // g + h)),
            pl.BlockSpec((None, n, HEAD_DIM), lambda b, h: (b, 0, BK_H0 + h)),
            pl.BlockSpec((None, n, HEAD_DIM), lambda b, h: (b, 0, BV_H0 + h)),
        ],
        out_specs=pl.BlockSpec((None, n, g * HEAD_DIM), lambda b, h: (b, 0, h)),
        out_shape=jax.ShapeDtypeStruct((bsz, n, BQ_W), BF16),
        compiler_params=_params(("parallel", "parallel")),
        name="swa",
    )(sink, z3, z3, z3)


def _dilated_group(z3, gi, *, tq=128):
    bsz, seq_len, _ = z3.shape
    wdw, d = DIL_PAIRS[gi]
    side = wdw // (2 * d)
    n = seq_len // d
    hp = DIL_HEADS_PER_GROUP
    zf = z3.reshape(bsz, n, d * IN_COLS)

    def in_spec(h0):
        return pl.BlockSpec((None, n, HEAD_DIM),
                            lambda b, r, h: (b, 0, r * IN_HEADS + h0 + gi * hp + h))

    out_spec = pl.BlockSpec((None, n, HEAD_DIM), lambda b, r, h: (b, 0, r * hp + h))
    o, lse = pl.pallas_call(
        functools.partial(_band_kernel, n=n, tq=tq, half=side, heads=1, has_sink=False,
                          want_lse=True),
        grid=(bsz, d, hp),
        in_specs=[in_spec(CQ_H0), in_spec(CK_H0), in_spec(CV_H0)],
        out_specs=[out_spec, out_spec],
        out_shape=[jax.ShapeDtypeStruct((bsz, n, d * hp * HEAD_DIM), BF16),
                   jax.ShapeDtypeStruct((bsz, n, d * hp * HEAD_DIM), F32)],
        compiler_params=_params(("parallel", "parallel", "parallel")),
        name=f"dilated{d}",
    )(zf, zf, zf)
    t = bsz * seq_len
    return o.reshape(t, hp * HEAD_DIM), lse.reshape(t, hp * HEAD_DIM)


def _outproj_kernel(ya_ref, gw_ref, gb_ref, yb_ref, o0_ref, o1_ref, o2_ref, l0_ref, l1_ref, l2_ref,
                    w_ref, g_ref, x_ref, out_ref):
    ya = ya_ref[...]
    gate = jnp.dot(ya, gw_ref[...], preferred_element_type=F32) + gb_ref[...]
    a = ya.astype(F32) * (1.0 / (1.0 + jnp.exp(-gate)))
    acc = jnp.dot(a.astype(BF16), w_ref[0:A_W, :], preferred_element_type=F32)
    acc += jnp.dot(yb_ref[...], w_ref[A_W:A_W + BQ_W, :], preferred_element_type=F32)
    l0, l1, l2 = l0_ref[...], l1_ref[...], l2_ref[...]
    mx = jnp.maximum(jnp.maximum(l0, l1), l2)
    e0, e1, e2 = jnp.exp(l0 - mx), jnp.exp(l1 - mx), jnp.exp(l2 - mx)
    yc = (e0 * o0_ref[...].astype(F32) + e1 * o1_ref[...].astype(F32)
          + e2 * o2_ref[...].astype(F32)) / (e0 + e1 + e2)
    acc += jnp.dot(yc.astype(BF16), w_ref[A_W + BQ_W:OUT_ROWS, :], preferred_element_type=F32)
    out_ref[...] = x_ref[...] + _rms(acc, g_ref[...])


def _outproj(ya, glu_w, glu_b, yb, os_, ls_, w_out, g_post, x, *, tm=256):
    t = x.shape[0]
    cw = DIL_HEADS_PER_GROUP * HEAD_DIM

    def rows(width):
        return pl.BlockSpec((tm, width), lambda i: (i, 0))

    def whole(shape):
        return pl.BlockSpec(shape, lambda i: (0, 0))

    return pl.pallas_call(
        _outproj_kernel,
        grid=(t // tm,),
        in_specs=[rows(A_W), whole((A_W, A_W)), whole((1, A_W)), rows(BQ_W),
                  rows(cw), rows(cw), rows(cw), rows(cw), rows(cw), rows(cw),
                  whole((OUT_ROWS, D_MODEL)), whole((1, D_MODEL)), rows(D_MODEL)],
        out_specs=rows(D_MODEL),
        out_shape=jax.ShapeDtypeStruct((t, D_MODEL), F32),
        compiler_params=_params(("parallel",)),
        name="outproj",
    )(ya, glu_w, glu_b, yb, *os_, *ls_, w_out, g_post, x)


def _ffn_kernel(xp_ref, x_ref, xq_ref, gpre_ref, wg_ref, wu_ref, cw_ref, cb_ref, wd_ref, gpost_ref,
                o_ref, xn_ref, g_ref, acc_ref, *, tm, tiles_per_seq):
    i = pl.program_id(0)
    j = pl.program_id(1)
    hb = BF16_ROWS

    @pl.when(j == 0)
    def _():
        gpre = gpre_ref[...]
        xn_ref[0:hb, :] = _rms(xp_ref[...], gpre).astype(BF16)
        xn_ref[hb:hb + tm, :] = _rms(x_ref[...], gpre).astype(BF16)
        xn_ref[hb + tm:2 * hb + tm, :] = _rms(xq_ref[...], gpre).astype(BF16)
        acc_ref[...] = jnp.zeros_like(acc_ref)

    g_ref[...] = jnp.dot(xn_ref[...], wg_ref[...], preferred_element_type=F32)
    up = jnp.dot(xn_ref[hb:hb + tm, :], wu_ref[...], preferred_element_type=F32)
    pos = i % tiles_per_seq
    row = lax.broadcasted_iota(jnp.int32, (tm, 1), 0)
    keep_prev = jnp.logical_or(row > 0, pos > 0)
    keep_next = jnp.logical_or(row < tm - 1, pos < tiles_per_seq - 1)
    g_prev = jnp.where(keep_prev, g_ref[hb - 1:hb - 1 + tm, :], 0.0)
    g_next = jnp.where(keep_next, g_ref[hb + 1:hb + 1 + tm, :], 0.0)
    g = (g_prev * cw_ref[0:1, :] + g_ref[hb:hb + tm, :] * cw_ref[1:2, :] + g_next * cw_ref[2:3, :]
         + cb_ref[...])
    h = _gelu_tanh(g) * up
    acc_ref[...] += jnp.dot(h.astype(BF16), wd_ref[...], preferred_element_type=F32)

    @pl.when(j == pl.num_programs(1) - 1)
    def _():
        o_ref[...] = x_ref[...] + _rms(acc_ref[...], gpost_ref[...])


def _ffn(x, g_pre, wg, wu, cw, cb, wd, g_post, seq_len, *, tm=512, tf=512):
    t = x.shape[0]
    hb = BF16_ROWS
    tiles_per_seq = seq_len // tm
    halo_blocks = t // hb
    r = tm // hb
    return pl.pallas_call(
        functools.partial(_ffn_kernel, tm=tm, tiles_per_seq=tiles_per_seq),
        grid=(t // tm, D_FF_PAD // tf),
        in_specs=[
            pl.BlockSpec((hb, D_MODEL), lambda i, j: (jnp.maximum(i * r - 1, 0), 0)),
            pl.BlockSpec((tm, D_MODEL), lambda i, j: (i, 0)),
            pl.BlockSpec((hb, D_MODEL), lambda i, j: (jnp.minimum((i + 1) * r, halo_blocks - 1), 0)),
            pl.BlockSpec((1, D_MODEL), lambda i, j: (0, 0)),
            pl.BlockSpec((D_MODEL, tf), lambda i, j: (0, j)),
            pl.BlockSpec((D_MODEL, tf), lambda i, j: (0, j)),
            pl.BlockSpec((3, tf), lambda i, j: (0, j)),
            pl.BlockSpec((1, tf), lambda i, j: (0, j)),
            pl.BlockSpec((tf, D_MODEL), lambda i, j: (j, 0)),
            pl.BlockSpec((1, D_MODEL), lambda i, j: (0, 0)),
        ],
        out_specs=pl.BlockSpec((tm, D_MODEL), lambda i, j: (i, 0)),
        out_shape=jax.ShapeDtypeStruct((t, D_MODEL), F32),
        scratch_shapes=[pltpu.VMEM((tm + 2 * hb, D_MODEL), BF16),
                        pltpu.VMEM((tm + 2 * hb, tf), F32),
                        pltpu.VMEM((tm, D_MODEL), F32)],
        compiler_params=_params(("parallel", "arbitrary")),
        name="ffn",
    )(x, x, x, g_pre, wg, wu, cw, cb, wd, g_post)


def _prepare_layer(l, ln_mix_pre, ln_mix_post, w_in, ssm_lam_re, ssm_lam_im, ssm_log_dt, ssm_b_re,
                   ssm_b_im, ssm_c_re, ssm_c_im, ssm_d, ssm_glu_w, ssm_glu_b, swa_sink, w_out,
                   ln_ffn_pre, ln_ffn_post, ffn_w_gate, ffn_w_up, ffn_conv_w, ffn_conv_b, ffn_w_down):
    fpad = D_FF_PAD - D_FF
    return dict(
        ln_mix_pre=ln_mix_pre[l].reshape(1, D_MODEL),
        ln_mix_post=ln_mix_post[l].reshape(1, D_MODEL),
        w_in=w_in[l].astype(BF16),
        s5=_s5_prep(ssm_lam_re[l], ssm_lam_im[l], ssm_log_dt[l], ssm_b_re[l], ssm_b_im[l],
                    ssm_c_re[l], ssm_c_im[l], ssm_d[l]),
        glu_w=ssm_glu_w[l].astype(BF16),
        glu_b=ssm_glu_b[l].reshape(1, SSM_CH),
        sink=swa_sink[l],
        w_out=w_out[l].astype(BF16),
        ln_ffn_pre=ln_ffn_pre[l].reshape(1, D_MODEL),
        ln_ffn_post=ln_ffn_post[l].reshape(1, D_MODEL),
        wg=jnp.pad(ffn_w_gate[l].astype(BF16), ((0, 0), (0, fpad))),
        wu=jnp.pad(ffn_w_up[l].astype(BF16), ((0, 0), (0, fpad))),
        cw=jnp.pad(ffn_conv_w[l], ((0, 0), (0, fpad))),
        cb=jnp.pad(ffn_conv_b[l].reshape(1, D_FF), ((0, 0), (0, fpad))),
        wd=jnp.pad(ffn_w_down[l].astype(BF16), ((0, fpad), (0, 0))),
    )


def _layer(x, bsz, seq_len, lp, tabs):
    z = _inproj(x, lp["ln_mix_pre"], lp["w_in"], tabs, seq_len)
    z3 = z.reshape(bsz, seq_len, IN_COLS)
    ya = _s5_branch(z, bsz, seq_len, lp["s5"])
    yb = _swa(z3, lp["sink"]).reshape(bsz * seq_len, BQ_W)
    dil = [_dilated_group(z3, gi) for gi in range(len(DIL_PAIRS))]
    x = _outproj(ya, lp["glu_w"], lp["glu_b"], yb, [o for o, _ in dil], [s for _, s in dil],
                 lp["w_out"], lp["ln_mix_post"], x)
    return _ffn(x, lp["ln_ffn_pre"], lp["wg"], lp["wu"], lp["cw"], lp["cb"], lp["wd"],
                lp["ln_ffn_post"], seq_len)


def kernel(x_prompt, x_sample, ln_mix_pre, ln_mix_post, w_in, ssm_lam_re, ssm_lam_im, ssm_log_dt, ssm_b_re, ssm_b_im, ssm_c_re, ssm_c_im, ssm_d, ssm_glu_w, ssm_glu_b, swa_sink, w_out, ln_ffn_pre, ln_ffn_post, ffn_w_gate, ffn_w_up, ffn_conv_w, ffn_conv_b, ffn_w_down):
    params = (ln_mix_pre, ln_mix_post, w_in, ssm_lam_re, ssm_lam_im, ssm_log_dt, ssm_b_re, ssm_b_im,
              ssm_c_re, ssm_c_im, ssm_d, ssm_glu_w, ssm_glu_b, swa_sink, w_out, ln_ffn_pre,
              ln_ffn_post, ffn_w_gate, ffn_w_up, ffn_conv_w, ffn_conv_b, ffn_w_down)
    depth = w_in.shape[0]
    layers = [_prepare_layer(l, *params) for l in range(depth)]
    outs = []
    for x in (x_prompt, x_sample):
        bsz, seq_len, _ = x.shape
        tabs = _rope_tables(seq_len)
        h = x.reshape(bsz * seq_len, D_MODEL)
        for lp in layers:
            h = _layer(h, bsz, seq_len, lp, tabs)
        outs.append(h.reshape(bsz, seq_len, D_MODEL))
    return tuple(outs)
```

```python
import functools
import math

import jax
import jax.numpy as jnp
from jax import lax
from jax.experimental import pallas as pl
from jax.experimental.pallas import tpu as pltpu

F32 = jnp.float32
BF16 = jnp.bfloat16

D_MODEL = 2048
HEAD_DIM = 128
SSM_CH = 768
SSM_GROUP = 16
SSM_GROUPS = SSM_CH // SSM_GROUP
SSM_STATE = 64
SWA_Q_HEADS = 4
SWA_KV_HEADS = 2
SWA_HALF = 128
DIL_PAIRS = ((128, 1), (512, 4), (2048, 16))
DIL_HEADS_PER_GROUP = 2
DIL_HEADS = DIL_HEADS_PER_GROUP * len(DIL_PAIRS)
ROPE_THETA = 500000.0
ROPE_DIM = HEAD_DIM // 4
D_FF = 7040
NORM_EPS = 1e-6
NEG_INF = -1e30

A_W = SSM_CH
BQ_W = SWA_Q_HEADS * HEAD_DIM
BKV_W = SWA_KV_HEADS * HEAD_DIM
C_W = DIL_HEADS * HEAD_DIM
YC_W = DIL_HEADS_PER_GROUP * HEAD_DIM
IN_COLS = A_W + BQ_W + 2 * BKV_W + 3 * C_W
OUT_ROWS = A_W + BQ_W + YC_W
BQ_H0 = A_W // HEAD_DIM
BK_H0 = BQ_H0 + SWA_Q_HEADS
BV_H0 = BK_H0 + SWA_KV_HEADS
CQ_H0 = BV_H0 + SWA_KV_HEADS
CK_H0 = CQ_H0 + DIL_HEADS
CV_H0 = CK_H0 + DIL_HEADS

LANES = 128
SUBLANES = 8
BF16_ROWS = 16
MXU_DIM = 256
VMEM_LIMIT = 56 << 20

TOKEN_TILE = 512
S5_CHUNK = SUBLANES
S5_BLOCK_GROUPS = LANES // SSM_GROUP
S5_BLOCKS = SSM_GROUPS // S5_BLOCK_GROUPS
S5_BLOCK_W = S5_CHUNK * LANES
S5_STATE_W = S5_BLOCK_GROUPS * SSM_STATE
S5_ROWS = 512
D_FF_PAD = -(-D_FF // (2 * MXU_DIM)) * (2 * MXU_DIM)


def _params(sem):
    return pltpu.CompilerParams(dimension_semantics=sem, vmem_limit_bytes=VMEM_LIMIT)


def _rms(x, g):
    ms = jnp.mean(x * x, axis=-1, keepdims=True)
    return x * lax.rsqrt(ms + NORM_EPS) * g


def _gelu_tanh(x):
    c = math.sqrt(2.0 / math.pi)
    return 0.5 * x * (1.0 + jnp.tanh(c * (x + 0.044715 * (x * x * x))))


def _padded_batch(bsz):
    return -(-bsz // SUBLANES) * SUBLANES


def _is_rope_head(h):
    return ((h >= BQ_H0) & (h < BV_H0)) | ((h >= CQ_H0) & (h < CV_H0))


def _inproj_kernel(x_ref, g_ref, w_ref, cos_ref, sa_ref, sb_ref, o_ref, u_ref, xn_ref, piece_ref,
                   *, bsz, tq, bp, heads_per_tile):
    j = pl.program_id(1)
    rows = bsz * tq
    q = S5_CHUNK
    chunks = tq // q

    @pl.when(j == 0)
    def _():
        xn_ref[...] = _rms(x_ref[...].reshape(rows, D_MODEL), g_ref[...]).astype(BF16)

    acc = jnp.dot(xn_ref[...], w_ref[...], preferred_element_type=F32)

    def per_seq(v):
        return v.reshape(bsz, tq, HEAD_DIM)

    for h in range(heads_per_tile):
        cols = slice(h * HEAD_DIM, (h + 1) * HEAD_DIM)
        zh = acc[:, cols]
        rope = _is_rope_head(j * heads_per_tile + h)

        @pl.when(rope)
        def _():
            r = (per_seq(zh) * cos_ref[...][None]
                 + per_seq(pltpu.roll(zh, HEAD_DIM - ROPE_DIM // 2, 1)) * sa_ref[...][None]
                 + per_seq(pltpu.roll(zh, ROPE_DIM // 2, 1)) * sb_ref[...][None])
            o_ref[:, :, cols] = r.astype(BF16)

        @pl.when(jnp.logical_not(rope))
        def _():
            o_ref[:, :, cols] = per_seq(zh).astype(BF16)

    u_heads = A_W // HEAD_DIM
    for jj in range(-(-u_heads // heads_per_tile)):
        @pl.when(j == jj)
        def _():
            for h in range(heads_per_tile):
                blk = jj * heads_per_tile + h
                if blk >= u_heads:
                    continue
                zh = acc[:, h * HEAD_DIM:(h + 1) * HEAD_DIM]
                if bp > bsz:
                    piece_ref[h] = jnp.zeros(piece_ref.shape[1:], F32)
                for c in range(chunks):
                    for b in range(bsz):
                        piece_ref[h, (c * bp + b) * q:(c * bp + b + 1) * q, :] = (
                            zh[b * tq + c * q:b * tq + (c + 1) * q, :])
                for t in range(q):
                    u_ref[:, blk * S5_BLOCK_W + t * LANES:blk * S5_BLOCK_W + (t + 1) * LANES] = (
                        piece_ref[h, pl.ds(t, chunks * bp, stride=q), :])


def _inproj(x3, g, w, tabs, *, tn=512):
    bsz, seq_len, _ = x3.shape
    tq = TOKEN_TILE // bsz
    bp = _padded_batch(bsz)
    chunks = tq // S5_CHUNK
    hpt = tn // HEAD_DIM
    cos_t, sa_t, sb_t = tabs
    tab_spec = pl.BlockSpec((tq, HEAD_DIM), lambda i, j: (i, 0))
    return pl.pallas_call(
        functools.partial(_inproj_kernel, bsz=bsz, tq=tq, bp=bp, heads_per_tile=hpt),
        grid=(seq_len // tq, IN_COLS // tn),
        in_specs=[
            pl.BlockSpec((bsz, tq, D_MODEL), lambda i, j: (0, i, 0)),
            pl.BlockSpec((1, D_MODEL), lambda i, j: (0, 0)),
            pl.BlockSpec((D_MODEL, tn), lambda i, j: (0, j)),
            tab_spec, tab_spec, tab_spec,
        ],
        out_specs=[
            pl.BlockSpec((bsz, tq, tn), lambda i, j: (0, i, j)),
            pl.BlockSpec((chunks * bp, S5_BLOCKS * S5_BLOCK_W), lambda i, j: (i, 0)),
        ],
        out_shape=[
            jax.ShapeDtypeStruct((bsz, seq_len, IN_COLS), BF16),
            jax.ShapeDtypeStruct((seq_len // S5_CHUNK * bp, S5_BLOCKS * S5_BLOCK_W), F32),
        ],
        scratch_shapes=[pltpu.VMEM((bsz * tq, D_MODEL), BF16),
                        pltpu.VMEM((hpt, chunks * bp * S5_CHUNK, LANES), F32)],
        compiler_params=_params(("parallel", "arbitrary")),
        name="inproj",
    )(x3, g, w, cos_t, sa_t, sb_t)


def _rope_tables(seq_len):
    half = ROPE_DIM // 2
    inv = ROPE_THETA ** (-jnp.arange(0, ROPE_DIM, 2, dtype=F32) / ROPE_DIM)
    ang = jnp.arange(seq_len, dtype=F32)[:, None] * inv[None, :]
    cos, sin = jnp.cos(ang), jnp.sin(ang)
    zeros = jnp.zeros((seq_len, HEAD_DIM - ROPE_DIM), F32)
    zh = jnp.zeros((seq_len, half), F32)
    cos_t = jnp.concatenate([cos, cos, jnp.ones((seq_len, HEAD_DIM - ROPE_DIM), F32)], axis=1)
    sa_t = jnp.concatenate([-sin, zh, zeros], axis=1)
    sb_t = jnp.concatenate([zh, sin, zeros], axis=1)
    return cos_t, sa_t, sb_t


def _s5prep_kernel(lr_ref, li_ref, ldt_ref, bre_ref, bim_ref, cre_ref, cim_ref, d_ref,
                   wt_ref, ct_ref, aq_ref, kt_ref):
    q = S5_CHUNK
    p = SSM_STATE
    kts = []
    for r in range(2):
        lr = lr_ref[r]
        li = li_ref[r]
        dt = jnp.exp(ldt_ref[r])
        mag = jnp.exp(lr * dt)
        ab_re = mag * jnp.cos(li * dt)
        ab_im = mag * jnp.sin(li * dt)
        nr = ab_re - 1.0
        den = lr * lr + li * li
        z_re = (nr * lr + ab_im * li) / den
        z_im = (ab_im * lr - nr * li) / den
        b_re = bre_ref[r]
        b_im = bim_ref[r]
        bb_re = z_re * b_re - z_im * b_im
        bb_im = z_re * b_im + z_im * b_re
        c_re = cre_ref[r]
        c_im = cim_ref[r]
        pw_re = jnp.ones((1, p), F32)
        pw_im = jnp.zeros((1, p), F32)
        for k in range(q + 1):
            if k < q:
                wt_ref[r, k, :, 0:p] = pw_re * bb_re - pw_im * bb_im
                wt_ref[r, k, :, p:2 * p] = pw_re * bb_im + pw_im * bb_re
            ct_ref[r, k, :, 0:p] = c_re * pw_re - c_im * pw_im
            ct_ref[r, k, :, p:2 * p] = -(c_re * pw_im + c_im * pw_re)
            if k < q:
                pw_re, pw_im = pw_re * ab_re - pw_im * ab_im, pw_re * ab_im + pw_im * ab_re
        aq_ref[r, :, 0:p] = pw_re
        aq_ref[r, :, p:2 * p] = pw_im
        wt_flat = wt_ref[r].reshape(q * SSM_GROUP, 2 * p)
        kts.append(lax.dot_general(wt_flat, ct_ref[r, 0], (((1,), (1,)), ((), ())),
                                   precision=lax.Precision.HIGHEST,
                                   preferred_element_type=F32))
    ktf, ktb = kts
    rows = lax.broadcasted_iota(jnp.int32, (SSM_GROUP, SSM_GROUP), 0)
    cols = lax.broadcasted_iota(jnp.int32, (SSM_GROUP, SSM_GROUP), 1)
    diag = jnp.where(rows == cols, jnp.broadcast_to(d_ref[...], (SSM_GROUP, SSM_GROUP)), 0.0)
    kt_ref[q - 1] = ktf[0:SSM_GROUP] + ktb[0:SSM_GROUP] + diag
    for k in range(1, q):
        blk = slice(k * SSM_GROUP, (k + 1) * SSM_GROUP)
        kt_ref[q - 1 + k] = ktf[blk]
        kt_ref[q - 1 - k] = ktb[blk]


def _s5_prep(lam_re, lam_im, log_dt, b_re, b_im, c_re, c_im, d_skip):
    g, p, c, q = SSM_GROUPS, SSM_STATE, SSM_GROUP, S5_CHUNK
    lr = lam_re.reshape(2, g, 1, p)
    li = lam_im.reshape(2, g, 1, p)
    ldt = jnp.broadcast_to(log_dt.reshape(2, g, 1, 1), (2, g, 1, p))
    bt_re = jnp.swapaxes(b_re, -1, -2)
    bt_im = jnp.swapaxes(b_im, -1, -2)
    d3 = d_skip.reshape(g, 1, c)
    vec_spec = pl.BlockSpec((2, None, 1, p), lambda i: (0, i, 0, 0))
    mat_spec = pl.BlockSpec((2, None, c, p), lambda i: (0, i, 0, 0))
    wt, ct, aq, kt = pl.pallas_call(
        _s5prep_kernel,
        grid=(g,),
        in_specs=[vec_spec, vec_spec, vec_spec, mat_spec, mat_spec, mat_spec, mat_spec,
                  pl.BlockSpec((None, 1, c), lambda i: (i, 0, 0))],
        out_specs=[
            pl.BlockSpec((None, 2, q, c, 2 * p), lambda i: (i, 0, 0, 0, 0)),
            pl.BlockSpec((None, 2, q + 1, c, 2 * p), lambda i: (i, 0, 0, 0, 0)),
            pl.BlockSpec((None, 2, 1, 2 * p), lambda i: (i, 0, 0, 0)),
            pl.BlockSpec((None, 2 * q - 1, c, c), lambda i: (i, 0, 0, 0)),
        ],
        out_shape=[
            jax.ShapeDtypeStruct((g, 2, q, c, 2 * p), F32),
            jax.ShapeDtypeStruct((g, 2, q + 1, c, 2 * p), F32),
            jax.ShapeDtypeStruct((g, 2, 1, 2 * p), F32),
            jax.ShapeDtypeStruct((g, 2 * q - 1, c, c), F32),
        ],
        compiler_params=_params(("parallel",)),
        name="s5prep",
    )(lr, li, ldt, bt_re, bt_im, c_re, c_im, d3)

    nb, gb = S5_BLOCKS, S5_BLOCK_GROUPS
    same = jnp.eye(gb, dtype=bool)

    def place(x, g_axis, new_axis):
        x = jnp.expand_dims(x, new_axis)
        shape = [1] * x.ndim
        shape[g_axis if g_axis < new_axis else g_axis + 1] = gb
        shape[new_axis] = gb
        return jnp.where(same.reshape(shape), x, 0.0)

    idx = jnp.arange(q)[None, :] - jnp.arange(q)[:, None] + (q - 1)
    k5 = kt[:, idx].reshape(nb, gb, q, q, c, c).transpose(0, 2, 1, 4, 3, 5)
    toep = place(k5, 2, 5).reshape(nb, S5_BLOCK_W, S5_BLOCK_W).astype(BF16)

    def in_op(w):
        w = w.reshape(nb, gb, q, c, 2, p).transpose(0, 2, 1, 3, 4, 5)
        return place(w, 2, 5).reshape(nb, S5_BLOCK_W, 2 * S5_STATE_W).astype(BF16)

    def out_op(w):
        w = w.reshape(nb, gb, q, c, 2, p).transpose(0, 4, 1, 5, 2, 3)
        return place(w, 2, 5).reshape(nb, 2 * S5_STATE_W, S5_BLOCK_W).astype(BF16)

    w1f = in_op(wt[:, 0, ::-1])
    w1b = in_op(wt[:, 1])
    w2f = out_op(ct[:, 0, 1:])
    w2b = out_op(ct[:, 1, 1:][:, ::-1])
    aqb = aq.reshape(nb, gb, 2, 2, p).transpose(0, 2, 3, 1, 4).reshape(nb, 2, 2, S5_STATE_W)
    return toep, w1f, w1b, w2f, w2b, aqb


def _chunk_scan(e_ref, s_ref, aq_ref, *, cb, bp, reverse):
    sw = S5_STATE_W
    a_re = jnp.broadcast_to(aq_ref[0:1, :], (bp, sw))
    a_im = jnp.broadcast_to(aq_ref[1:2, :], (bp, sw))

    def body(i, carry):
        s_re, s_im = carry
        c = (cb - 1 - i) if reverse else i
        rows = pl.ds(pl.multiple_of(c * bp, bp), bp)
        e_re = e_ref[rows, 0:sw]
        e_im = e_ref[rows, sw:2 * sw]
        e_ref[rows, 0:sw] = s_re
        e_ref[rows, sw:2 * sw] = s_im
        return a_re * s_re - a_im * s_im + e_re, a_re * s_im + a_im * s_re + e_im

    s_re, s_im = lax.fori_loop(0, cb, body, (s_ref[:, 0:sw], s_ref[:, sw:2 * sw]))
    s_ref[:, 0:sw] = s_re
    s_ref[:, sw:2 * sw] = s_im


def _s5_bwd_kernel(u_ref, w1_ref, aq_ref, r_ref, e_ref, s_ref, *, cb, bp):
    @pl.when(pl.program_id(1) == 0)
    def _():
        s_ref[...] = jnp.zeros_like(s_ref)

    e_ref[...] = jnp.dot(u_ref[...].astype(BF16), w1_ref[...], preferred_element_type=F32)
    _chunk_scan(e_ref, s_ref, aq_ref, cb=cb, bp=bp, reverse=True)
    r_ref[...] = e_ref[...].astype(BF16)


def _s5_fwd_kernel(u_ref, rin_ref, t_ref, w1_ref, w2f_ref, w2b_ref, aq_ref, y_ref, e_ref, s_ref,
                   *, cb, bp):
    @pl.when(pl.program_id(1) == 0)
    def _():
        s_ref[...] = jnp.zeros_like(s_ref)

    ub = u_ref[...].astype(BF16)
    e_ref[...] = jnp.dot(ub, w1_ref[...], preferred_element_type=F32)
    _chunk_scan(e_ref, s_ref, aq_ref, cb=cb, bp=bp, reverse=False)
    y = jnp.dot(ub, t_ref[...], preferred_element_type=F32)
    y += jnp.dot(e_ref[...].astype(BF16), w2f_ref[...], preferred_element_type=F32)
    y += jnp.dot(rin_ref[...], w2b_ref[...], preferred_element_type=F32)
    y = _gelu_tanh(y)
    for t in range(S5_CHUNK):
        y_ref[pl.ds(t, cb * bp, stride=S5_CHUNK), :] = y[:, t * LANES:(t + 1) * LANES]


def _s5_mix(uc, ops, *, bp):
    toep, w1f, w1b, w2f, w2b, aqb = ops
    m = uc.shape[0]
    rows = S5_ROWS
    cb = rows // bp
    nblk = m // rows
    bw, sw2 = S5_BLOCK_W, 2 * S5_STATE_W

    def wspec(r, c):
        return pl.BlockSpec((None, r, c), lambda g, s: (g, 0, 0))

    rin = pl.pallas_call(
        functools.partial(_s5_bwd_kernel, cb=cb, bp=bp),
        grid=(S5_BLOCKS, nblk),
        in_specs=[pl.BlockSpec((rows, bw), lambda g, s: (nblk - 1 - s, g)),
                  wspec(bw, sw2),
                  pl.BlockSpec((None, None, 2, S5_STATE_W), lambda g, s: (g, 1, 0, 0))],
        out_specs=pl.BlockSpec((rows, sw2), lambda g, s: (nblk - 1 - s, g)),
        out_shape=jax.ShapeDtypeStruct((m, S5_BLOCKS * sw2), BF16),
        scratch_shapes=[pltpu.VMEM((rows, sw2), F32), pltpu.VMEM((bp, sw2), F32)],
        compiler_params=_params(("parallel", "arbitrary")),
        name="s5bwd",
    )(uc, w1b, aqb)

    return pl.pallas_call(
        functools.partial(_s5_fwd_kernel, cb=cb, bp=bp),
        grid=(S5_BLOCKS, nblk),
        in_specs=[pl.BlockSpec((rows, bw), lambda g, s: (s, g)),
                  pl.BlockSpec((rows, sw2), lambda g, s: (s, g)),
                  wspec(bw, bw), wspec(bw, sw2), wspec(sw2, bw), wspec(sw2, bw),
                  pl.BlockSpec((None, None, 2, S5_STATE_W), lambda g, s: (g, 0, 0, 0))],
        out_specs=pl.BlockSpec((rows * S5_CHUNK, LANES), lambda g, s: (s, g)),
        out_shape=jax.ShapeDtypeStruct((m * S5_CHUNK, SSM_CH), F32),
        scratch_shapes=[pltpu.VMEM((rows, sw2), F32), pltpu.VMEM((bp, sw2), F32)],
        compiler_params=_params(("parallel", "arbitrary")),
        name="s5fwd",
    )(uc, rin, toep, w1f, w2f, w2b, aqb)


def _attend(q, k, v, mask, sink):
    s = lax.dot_general(q, k, (((1,), (1,)), ((), ())), preferred_element_type=F32)
    s = jnp.where(mask, s * (HEAD_DIM ** -0.5), NEG_INF)
    mx = jnp.max(s, axis=-1, keepdims=True)
    if sink is not None:
        mx = jnp.maximum(mx, sink)
    p = jnp.exp(s - mx)
    den = jnp.sum(p, axis=-1, keepdims=True)
    if sink is not None:
        den = den + jnp.exp(sink - mx)
    o = jnp.dot(p.astype(BF16), v, preferred_element_type=F32) / den
    return o, mx + jnp.log(den)


def _band_window(n, tq, half):
    w = min(tq + 2 * half, n)
    align = math.gcd(math.gcd(tq, half), n - w) if n > w else tq
    return w, align


def _swa_kernel(sink_ref, q_ref, k_ref, v_ref, o_ref, *, n, tq, heads):
    half = SWA_HALF
    w, align = _band_window(n, tq, half)
    sinks = [sink_ref[pl.program_id(1) * heads + g] for g in range(heads)]
    col = lax.broadcasted_iota(jnp.int32, (tq, w), 1)
    row = lax.broadcasted_iota(jnp.int32, (tq, w), 0)

    def body(i, _):
        q0 = pl.multiple_of(i * tq, tq)
        start = pl.multiple_of(jnp.clip(q0 - half, 0, n - w), align)
        k = k_ref[pl.ds(start, w), :]
        v = v_ref[pl.ds(start, w), :]
        mask = jnp.abs(col - row + (start - q0)) <= half
        for g in range(heads):
            cols = slice(g * HEAD_DIM, (g + 1) * HEAD_DIM)
            o, _ = _attend(q_ref[pl.ds(q0, tq), cols], k, v, mask, sinks[g])
            o_ref[pl.ds(q0, tq), cols] = o.astype(o_ref.dtype)
        return 0

    lax.fori_loop(0, n // tq, body, 0)


def _swa(z3, sink, *, tq=128):
    bsz, n, _ = z3.shape
    g = SWA_Q_HEADS // SWA_KV_HEADS
    return pl.pallas_call(
        functools.partial(_swa_kernel, n=n, tq=tq, heads=g),
        grid=(bsz, SWA_KV_HEADS),
        in_specs=[
            pl.BlockSpec(memory_space=pltpu.SMEM),
            pl.BlockSpec((None, n, g * HEAD_DIM), lambda b, h: (b, 0, BQ_H0 // g + h)),
            pl.BlockSpec((None, n, HEAD_DIM), lambda b, h: (b, 0, BK_H0 + h)),
            pl.BlockSpec((None, n, HEAD_DIM), lambda b, h: (b, 0, BV_H0 + h)),
        ],
        out_specs=pl.BlockSpec((None, n, g * HEAD_DIM), lambda b, h: (b, 0, h)),
        out_shape=jax.ShapeDtypeStruct((bsz, n, BQ_W), BF16),
        compiler_params=_params(("parallel", "parallel")),
        name="swa",
    )(sink, z3, z3, z3)


def _dilated_kernel(*refs, seq_len, tq):
    ng = len(DIL_PAIRS)
    qkv = [refs[3 * g:3 * g + 3] for g in range(ng)]
    o_ref = refs[3 * ng]
    stage_ref, qf_ref, kf_ref, vf_ref, uo_ref, ul_ref = refs[3 * ng + 1:]

    for gi, (wdw, d) in enumerate(DIL_PAIRS):
        half = wdw // (2 * d)
        n = seq_len // d
        w, align = _band_window(n, tq, half)
        if d > 1:
            for src, dst in zip(qkv[gi], (qf_ref, kf_ref, vf_ref)):
                stage_ref[...] = src[...].astype(F32)
                for r in range(d):
                    dst[r * n:(r + 1) * n, :] = stage_ref[pl.ds(r, n, stride=d), :].astype(BF16)
            q_ref, k_ref, v_ref = qf_ref, kf_ref, vf_ref
        else:
            q_ref, k_ref, v_ref = qkv[gi]
        col = lax.broadcasted_iota(jnp.int32, (tq, w), 1)
        row = lax.broadcasted_iota(jnp.int32, (tq, w), 0)
        per_res = n // tq

        def body(blk, _, d=d, n=n, w=w, align=align, half=half, per_res=per_res, gi=gi,
                 q_ref=q_ref, k_ref=k_ref, v_ref=v_ref, col=col, row=row):
            r = blk // per_res
            q0 = pl.multiple_of((blk % per_res) * tq, tq)
            base = pl.multiple_of(r * n, tq)
            start = pl.multiple_of(jnp.clip(q0 - half, 0, n - w), align)
            k = k_ref[pl.ds(base + start, w), :]
            v = v_ref[pl.ds(base + start, w), :]
            mask = jnp.abs(col - row + (start - q0)) <= half
            o, lse = _attend(q_ref[pl.ds(base + q0, tq), :], k, v, mask, None)
            lse = jnp.broadcast_to(lse, (tq, HEAD_DIM))
            if d > 1:
                rows = pl.ds(q0 * d + r, tq, stride=d)
            else:
                rows = pl.ds(q0, tq)
            uo_ref[gi, rows, :] = o
            ul_ref[gi, rows, :] = lse
            return 0

        lax.fori_loop(0, seq_len // tq, body, 0)

    def merge(i, _):
        rows = pl.ds(pl.multiple_of(i * tq, tq), tq)
        ls = [ul_ref[g, rows, :] for g in range(ng)]
        mx = functools.reduce(jnp.maximum, ls)
        es = [jnp.exp(l - mx) for l in ls]
        num = sum(e * uo_ref[g, rows, :] for g, e in enumerate(es))
        o_ref[rows, :] = (num / sum(es)).astype(o_ref.dtype)
        return 0

    lax.fori_loop(0, seq_len // tq, merge, 0)


def _dilated(z3, *, tq=128):
    bsz, seq_len, _ = z3.shape
    hp = DIL_HEADS_PER_GROUP
    ng = len(DIL_PAIRS)

    def head_spec(h0, gi):
        return pl.BlockSpec((None, seq_len, HEAD_DIM), lambda b, h: (b, 0, h0 + gi * hp + h))

    in_specs = [head_spec(h0, gi) for gi in range(ng) for h0 in (CQ_H0, CK_H0, CV_H0)]
    return pl.pallas_call(
        functools.partial(_dilated_kernel, seq_len=seq_len, tq=tq),
        grid=(bsz, hp),
        in_specs=in_specs,
        out_specs=pl.BlockSpec((None, seq_len, HEAD_DIM), lambda b, h: (b, 0, h)),
        out_shape=jax.ShapeDtypeStruct((bsz, seq_len, YC_W), BF16),
        scratch_shapes=[pltpu.VMEM((seq_len, HEAD_DIM), F32),
                        pltpu.VMEM((seq_len, HEAD_DIM), BF16),
                        pltpu.VMEM((seq_len, HEAD_DIM), BF16),
                        pltpu.VMEM((seq_len, HEAD_DIM), BF16),
                        pltpu.VMEM((ng, seq_len, HEAD_DIM), F32),
                        pltpu.VMEM((ng, seq_len, HEAD_DIM), F32)],
        compiler_params=_params(("parallel", "parallel")),
        name="dilated",
    )(*([z3] * (3 * ng)))


def _outproj_kernel(ya_ref, gw_ref, gb_ref, yb_ref, yc_ref, w_ref, g_ref, x_ref, out_ref, ya_scr,
                    *, bsz, tq, bp):
    rows = bsz * tq
    q = S5_CHUNK
    chunks = tq // q
    for c in range(chunks):
        for b in range(bsz):
            ya_scr[b * tq + c * q:b * tq + (c + 1) * q, :] = ya_ref[(c * bp + b) * q:(c * bp + b + 1) * q, :]
    ya = ya_scr[...]
    gate = jnp.dot(ya.astype(BF16), gw_ref[...], preferred_element_type=F32) + gb_ref[...]
    a = ya * (1.0 / (1.0 + jnp.exp(-gate)))
    acc = jnp.dot(a.astype(BF16), w_ref[0:A_W, :], preferred_element_type=F32)
    acc += jnp.dot(yb_ref[...].reshape(rows, BQ_W), w_ref[A_W:A_W + BQ_W, :],
                   preferred_element_type=F32)
    acc += jnp.dot(yc_ref[...].reshape(rows, YC_W), w_ref[A_W + BQ_W:OUT_ROWS, :],
                   preferred_element_type=F32)
    x = x_ref[...].reshape(rows, D_MODEL)
    out_ref[...] = (x + _rms(acc, g_ref[...])).reshape(bsz, tq, D_MODEL)


def _outproj(ya, glu_w, glu_b, yb, yc, w_out, g_post, x3):
    bsz, seq_len, _ = x3.shape
    tq = TOKEN_TILE // bsz
    bp = _padded_batch(bsz)

    def tile(width):
        return pl.BlockSpec((bsz, tq, width), lambda i: (0, i, 0))

    def whole(shape):
        return pl.BlockSpec(shape, lambda i: (0, 0))

    return pl.pallas_call(
        functools.partial(_outproj_kernel, bsz=bsz, tq=tq, bp=bp),
        grid=(seq_len // tq,),
        in_specs=[pl.BlockSpec((tq * bp, A_W), lambda i: (i, 0)),
                  whole((A_W, A_W)), whole((1, A_W)), tile(BQ_W), tile(YC_W),
                  whole((OUT_ROWS, D_MODEL)), whole((1, D_MODEL)), tile(D_MODEL)],
        out_specs=tile(D_MODEL),
        out_shape=jax.ShapeDtypeStruct((bsz, seq_len, D_MODEL), F32),
        scratch_shapes=[pltpu.VMEM((bsz * tq, A_W), F32)],
        compiler_params=_params(("parallel",)),
        name="outproj",
    )(ya, glu_w, glu_b, yb, yc, w_out, g_post, x3)


def _ffn_kernel(xp_ref, x_ref, xq_ref, gpre_ref, wg_ref, wu_ref, cw_ref, cb_ref, wd_ref, gpost_ref,
                o_ref, xn_ref, g_ref, acc_ref, *, tm, tiles_per_seq):
    i = pl.program_id(0)
    j = pl.program_id(1)
    hb = BF16_ROWS

    @pl.when(j == 0)
    def _():
        gpre = gpre_ref[...]
        xn_ref[0:hb, :] = _rms(xp_ref[...], gpre).astype(BF16)
        xn_ref[hb:hb + tm, :] = _rms(x_ref[...], gpre).astype(BF16)
        xn_ref[hb + tm:2 * hb + tm, :] = _rms(xq_ref[...], gpre).astype(BF16)
        acc_ref[...] = jnp.zeros_like(acc_ref)

    g_ref[...] = jnp.dot(xn_ref[...], wg_ref[...], preferred_element_type=F32)
    up = jnp.dot(xn_ref[hb:hb + tm, :], wu_ref[...], preferred_element_type=F32)
    pos = i % tiles_per_seq
    row = lax.broadcasted_iota(jnp.int32, (tm, 1), 0)
    keep_prev = jnp.logical_or(row > 0, pos > 0)
    keep_next = jnp.logical_or(row < tm - 1, pos < tiles_per_seq - 1)
    g_prev = jnp.where(keep_prev, g_ref[hb - 1:hb - 1 + tm, :], 0.0)
    g_next = jnp.where(keep_next, g_ref[hb + 1:hb + 1 + tm, :], 0.0)
    g = (g_prev * cw_ref[0:1, :] + g_ref[hb:hb + tm, :] * cw_ref[1:2, :] + g_next * cw_ref[2:3, :]
         + cb_ref[...])
    h = _gelu_tanh(g) * up
    acc_ref[...] += jnp.dot(h.astype(BF16), wd_ref[...], preferred_element_type=F32)

    @pl.when(j == pl.num_programs(1) - 1)
    def _():
        o_ref[...] = x_ref[...] + _rms(acc_ref[...], gpost_ref[...])


def _ffn(x, g_pre, wg, wu, cw, cb, wd, g_post, seq_len, *, tm=TOKEN_TILE, tf=512):
    t = x.shape[0]
    hb = BF16_ROWS
    tiles_per_seq = seq_len // tm
    halo_blocks = t // hb
    r = tm // hb
    return pl.pallas_call(
        functools.partial(_ffn_kernel, tm=tm, tiles_per_seq=tiles_per_seq),
        grid=(t // tm, D_FF_PAD // tf),
        in_specs=[
            pl.BlockSpec((hb, D_MODEL), lambda i, j: (jnp.maximum(i * r - 1, 0), 0)),
            pl.BlockSpec((tm, D_MODEL), lambda i, j: (i, 0)),
            pl.BlockSpec((hb, D_MODEL), lambda i, j: (jnp.minimum((i + 1) * r, halo_blocks - 1), 0)),
            pl.BlockSpec((1, D_MODEL), lambda i, j: (0, 0)),
            pl.BlockSpec((D_MODEL, tf), lambda i, j: (0, j)),
            pl.BlockSpec((D_MODEL, tf), lambda i, j: (0, j)),
            pl.BlockSpec((3, tf), lambda i, j: (0, j)),
            pl.BlockSpec((1, tf), lambda i, j: (0, j)),
            pl.BlockSpec((tf, D_MODEL), lambda i, j: (j, 0)),
            pl.BlockSpec((1, D_MODEL), lambda i, j: (0, 0)),
        ],
        out_specs=pl.BlockSpec((tm, D_MODEL), lambda i, j: (i, 0)),
        out_shape=jax.ShapeDtypeStruct((t, D_MODEL), F32),
        scratch_shapes=[pltpu.VMEM((tm + 2 * hb, D_MODEL), BF16),
                        pltpu.VMEM((tm + 2 * hb, tf), F32),
                        pltpu.VMEM((tm, D_MODEL), F32)],
        compiler_params=_params(("parallel", "arbitrary")),
        name="ffn",
    )(x, x, x, g_pre, wg, wu, cw, cb, wd, g_post)


def _prepare_layer(l, ln_mix_pre, ln_mix_post, w_in, ssm_lam_re, ssm_lam_im, ssm_log_dt, ssm_b_re,
                   ssm_b_im, ssm_c_re, ssm_c_im, ssm_d, ssm_glu_w, ssm_glu_b, swa_sink, w_out,
                   ln_ffn_pre, ln_ffn_post, ffn_w_gate, ffn_w_up, ffn_conv_w, ffn_conv_b, ffn_w_down):
    fpad = D_FF_PAD - D_FF
    return dict(
        ln_mix_pre=ln_mix_pre[l].reshape(1, D_MODEL),
        ln_mix_post=ln_mix_post[l].reshape(1, D_MODEL),
        w_in=w_in[l].astype(BF16),
        s5=_s5_prep(ssm_lam_re[l], ssm_lam_im[l], ssm_log_dt[l], ssm_b_re[l], ssm_b_im[l],
                    ssm_c_re[l], ssm_c_im[l], ssm_d[l]),
        glu_w=ssm_glu_w[l].astype(BF16),
        glu_b=ssm_glu_b[l].reshape(1, SSM_CH),
        sink=swa_sink[l],
        w_out=w_out[l].astype(BF16),
        ln_ffn_pre=ln_ffn_pre[l].reshape(1, D_MODEL),
        ln_ffn_post=ln_ffn_post[l].reshape(1, D_MODEL),
        wg=jnp.pad(ffn_w_gate[l].astype(BF16), ((0, 0), (0, fpad))),
        wu=jnp.pad(ffn_w_up[l].astype(BF16), ((0, 0), (0, fpad))),
        cw=jnp.pad(ffn_conv_w[l], ((0, 0), (0, fpad))),
        cb=jnp.pad(ffn_conv_b[l].reshape(1, D_FF), ((0, 0), (0, fpad))),
        wd=jnp.pad(ffn_w_down[l].astype(BF16), ((0, fpad), (0, 0))),
    )


def _layer(x3, lp, tabs):
    bsz, seq_len, _ = x3.shape
    z3, uc = _inproj(x3, lp["ln_mix_pre"], lp["w_in"], tabs)
    ya = _s5_mix(uc, lp["s5"], bp=_padded_batch(bsz))
    yb = _swa(z3, lp["sink"])
    yc = _dilated(z3)
    x3 = _outproj(ya, lp["glu_w"], lp["glu_b"], yb, yc, lp["w_out"], lp["ln_mix_post"], x3)
    x = _ffn(x3.reshape(bsz * seq_len, D_MODEL), lp["ln_ffn_pre"], lp["wg"], lp["wu"], lp["cw"],
             lp["cb"], lp["wd"], lp["ln_ffn_post"], seq_len)
    return x.reshape(bsz, seq_len, D_MODEL)


def kernel(x_prompt, x_sample, ln_mix_pre, ln_mix_post, w_in, ssm_lam_re, ssm_lam_im, ssm_log_dt, ssm_b_re, ssm_b_im, ssm_c_re, ssm_c_im, ssm_d, ssm_glu_w, ssm_glu_b, swa_sink, w_out, ln_ffn_pre, ln_ffn_post, ffn_w_gate, ffn_w_up, ffn_conv_w, ffn_conv_b, ffn_w_down):
    params = (ln_mix_pre, ln_mix_post, w_in, ssm_lam_re, ssm_lam_im, ssm_log_dt, ssm_b_re, ssm_b_im,
              ssm_c_re, ssm_c_im, ssm_d, ssm_glu_w, ssm_glu_b, swa_sink, w_out, ln_ffn_pre,
              ln_ffn_post, ffn_w_gate, ffn_w_up, ffn_conv_w, ffn_conv_b, ffn_w_down)
    depth = w_in.shape[0]
    layers = [_prepare_layer(l, *params) for l in range(depth)]
    outs = []
    for x3 in (x_prompt, x_sample):
        tabs = _rope_tables(x3.shape[1])
        for lp in layers:
            x3 = _layer(x3, lp, tabs)
        outs.append(x3)
    return tuple(outs)
```

```python
import functools
import math

import jax
import jax.numpy as jnp
from jax import lax
from jax.experimental import pallas as pl
from jax.experimental.pallas import tpu as pltpu

F32 = jnp.float32
BF16 = jnp.bfloat16

D_MODEL = 2048
HEAD_DIM = 128
SSM_CH = 768
SSM_GROUP = 16
SSM_GROUPS = SSM_CH // SSM_GROUP
SSM_STATE = 64
SWA_Q_HEADS = 4
SWA_KV_HEADS = 2
SWA_HALF = 128
DIL_PAIRS = ((128, 1), (512, 4), (2048, 16))
DIL_HEADS_PER_GROUP = 2
DIL_HEADS = DIL_HEADS_PER_GROUP * len(DIL_PAIRS)
ROPE_THETA = 500000.0
ROPE_DIM = HEAD_DIM // 4
D_FF = 7040
NORM_EPS = 1e-6
NEG_INF = -1e30

A_W = SSM_CH
BQ_W = SWA_Q_HEADS * HEAD_DIM
BKV_W = SWA_KV_HEADS * HEAD_DIM
C_W = DIL_HEADS * HEAD_DIM
YC_W = DIL_HEADS_PER_GROUP * HEAD_DIM
IN_COLS = A_W + BQ_W + 2 * BKV_W + 3 * C_W
OUT_ROWS = A_W + BQ_W + YC_W
BQ_H0 = A_W // HEAD_DIM
BK_H0 = BQ_H0 + SWA_Q_HEADS
BV_H0 = BK_H0 + SWA_KV_HEADS
CQ_H0 = BV_H0 + SWA_KV_HEADS
CK_H0 = CQ_H0 + DIL_HEADS
CV_H0 = CK_H0 + DIL_HEADS

LANES = 128
SUBLANES = 8
BF16_ROWS = 16
MXU_DIM = 256
VMEM_LIMIT = 56 << 20

TOKEN_TILE = 512
S5_CHUNK = SUBLANES
S5_BLOCK_GROUPS = LANES // SSM_GROUP
S5_BLOCKS = SSM_GROUPS // S5_BLOCK_GROUPS
S5_BLOCK_W = S5_CHUNK * LANES
S5_STATE_W = S5_BLOCK_GROUPS * SSM_STATE
S5_ROWS = 512
ATTN_UNROLL = 4
D_FF_PAD = -(-D_FF // (2 * MXU_DIM)) * (2 * MXU_DIM)


def _params(sem):
    return pltpu.CompilerParams(dimension_semantics=sem, vmem_limit_bytes=VMEM_LIMIT)


def _rms(x, g):
    ms = jnp.mean(x * x, axis=-1, keepdims=True)
    return x * lax.rsqrt(ms + NORM_EPS) * g


def _gelu_tanh(x):
    c = math.sqrt(2.0 / math.pi)
    return 0.5 * x * (1.0 + jnp.tanh(c * (x + 0.044715 * (x * x * x))))


def _padded_batch(bsz):
    return -(-bsz // SUBLANES) * SUBLANES


def _is_rope_head(h):
    return BQ_H0 <= h < BV_H0 or CQ_H0 <= h < CV_H0


def _inproj_kernel(x_ref, g_ref, w_ref, cos_ref, sa_ref, sb_ref, o_ref, u_ref, xn_ref, piece_ref,
                   *, bsz, tq, bp, tn):
    rows = bsz * tq
    q = S5_CHUNK
    chunks = tq // q
    heads_per_tile = tn // HEAD_DIM
    u_heads = A_W // HEAD_DIM
    xn_ref[...] = _rms(x_ref[...].reshape(rows, D_MODEL), g_ref[...]).astype(BF16)

    def per_seq(v):
        return v.reshape(bsz, tq, HEAD_DIM)

    for jt in range(IN_COLS // tn):
        acc = jnp.dot(xn_ref[...], w_ref[:, jt * tn:(jt + 1) * tn], preferred_element_type=F32)
        for h in range(heads_per_tile):
            head = jt * heads_per_tile + h
            cols = slice(head * HEAD_DIM, (head + 1) * HEAD_DIM)
            zh = acc[:, h * HEAD_DIM:(h + 1) * HEAD_DIM]
            if _is_rope_head(head):
                r = (per_seq(zh) * cos_ref[...][None]
                     + per_seq(pltpu.roll(zh, HEAD_DIM - ROPE_DIM // 2, 1)) * sa_ref[...][None]
                     + per_seq(pltpu.roll(zh, ROPE_DIM // 2, 1)) * sb_ref[...][None])
                o_ref[:, :, cols] = r.astype(BF16)
            else:
                o_ref[:, :, cols] = per_seq(zh).astype(BF16)
            if head < u_heads:
                if bp > bsz:
                    piece_ref[head] = jnp.zeros(piece_ref.shape[1:], F32)
                for c in range(chunks):
                    for b in range(bsz):
                        piece_ref[head, (c * bp + b) * q:(c * bp + b + 1) * q, :] = (
                            zh[b * tq + c * q:b * tq + (c + 1) * q, :])
                for t in range(q):
                    u_ref[:, head * S5_BLOCK_W + t * LANES:head * S5_BLOCK_W + (t + 1) * LANES] = (
                        piece_ref[head, pl.ds(t, chunks * bp, stride=q), :])


def _inproj(x3, g_all, w_all, layer, tabs, *, tn=512):
    bsz, seq_len, _ = x3.shape
    tq = TOKEN_TILE // bsz
    bp = _padded_batch(bsz)
    chunks = tq // S5_CHUNK
    cos_t, sa_t, sb_t = tabs
    tab_spec = pl.BlockSpec((tq, HEAD_DIM), lambda i: (i, 0))
    return pl.pallas_call(
        functools.partial(_inproj_kernel, bsz=bsz, tq=tq, bp=bp, tn=tn),
        grid=(seq_len // tq,),
        in_specs=[
            pl.BlockSpec((bsz, tq, D_MODEL), lambda i: (0, i, 0)),
            pl.BlockSpec((None, 1, D_MODEL), lambda i: (layer, 0, 0)),
            pl.BlockSpec((None, D_MODEL, IN_COLS), lambda i: (layer, 0, 0),
                         pipeline_mode=pl.Buffered(1)),
            tab_spec, tab_spec, tab_spec,
        ],
        out_specs=[
            pl.BlockSpec((bsz, tq, IN_COLS), lambda i: (0, i, 0)),
            pl.BlockSpec((chunks * bp, S5_BLOCKS * S5_BLOCK_W), lambda i: (i, 0)),
        ],
        out_shape=[
            jax.ShapeDtypeStruct((bsz, seq_len, IN_COLS), BF16),
            jax.ShapeDtypeStruct((seq_len // S5_CHUNK * bp, S5_BLOCKS * S5_BLOCK_W), F32),
        ],
        scratch_shapes=[pltpu.VMEM((bsz * tq, D_MODEL), BF16),
                        pltpu.VMEM((A_W // HEAD_DIM, chunks * bp * S5_CHUNK, LANES), F32)],
        compiler_params=_params(("parallel",)),
        name="inproj",
    )(x3, g_all, w_all, cos_t, sa_t, sb_t)


def _rope_tables(seq_len):
    half = ROPE_DIM // 2
    inv = ROPE_THETA ** (-jnp.arange(0, ROPE_DIM, 2, dtype=F32) / ROPE_DIM)
    ang = jnp.arange(seq_len, dtype=F32)[:, None] * inv[None, :]
    cos, sin = jnp.cos(ang), jnp.sin(ang)
    zeros = jnp.zeros((seq_len, HEAD_DIM - ROPE_DIM), F32)
    zh = jnp.zeros((seq_len, half), F32)
    cos_t = jnp.concatenate([cos, cos, jnp.ones((seq_len, HEAD_DIM - ROPE_DIM), F32)], axis=1)
    sa_t = jnp.concatenate([-sin, zh, zeros], axis=1)
    sb_t = jnp.concatenate([zh, sin, zeros], axis=1)
    return cos_t, sa_t, sb_t


def _s5prep_kernel(lr_ref, li_ref, ldt_ref, bre_ref, bim_ref, cre_ref, cim_ref, d_ref,
                   wt_ref, ct_ref, aq_ref, kt_ref):
    q = S5_CHUNK
    p = SSM_STATE
    kts = []
    for r in range(2):
        lr = lr_ref[r]
        li = li_ref[r]
        dt = jnp.exp(ldt_ref[r])
        mag = jnp.exp(lr * dt)
        ab_re = mag * jnp.cos(li * dt)
        ab_im = mag * jnp.sin(li * dt)
        nr = ab_re - 1.0
        den = lr * lr + li * li
        z_re = (nr * lr + ab_im * li) / den
        z_im = (ab_im * lr - nr * li) / den
        b_re = bre_ref[r]
        b_im = bim_ref[r]
        bb_re = z_re * b_re - z_im * b_im
        bb_im = z_re * b_im + z_im * b_re
        c_re = cre_ref[r]
        c_im = cim_ref[r]
        pw_re = jnp.ones((1, p), F32)
        pw_im = jnp.zeros((1, p), F32)
        for k in range(q + 1):
            if k < q:
                wt_ref[r, k, :, 0:p] = pw_re * bb_re - pw_im * bb_im
                wt_ref[r, k, :, p:2 * p] = pw_re * bb_im + pw_im * bb_re
            ct_ref[r, k, :, 0:p] = c_re * pw_re - c_im * pw_im
            ct_ref[r, k, :, p:2 * p] = -(c_re * pw_im + c_im * pw_re)
            if k < q:
                pw_re, pw_im = pw_re * ab_re - pw_im * ab_im, pw_re * ab_im + pw_im * ab_re
        aq_ref[r, :, 0:p] = pw_re
        aq_ref[r, :, p:2 * p] = pw_im
        wt_flat = wt_ref[r].reshape(q * SSM_GROUP, 2 * p)
        kts.append(lax.dot_general(wt_flat, ct_ref[r, 0], (((1,), (1,)), ((), ())),
                                   precision=lax.Precision.HIGHEST,
                                   preferred_element_type=F32))
    ktf, ktb = kts
    rows = lax.broadcasted_iota(jnp.int32, (SSM_GROUP, SSM_GROUP), 0)
    cols = lax.broadcasted_iota(jnp.int32, (SSM_GROUP, SSM_GROUP), 1)
    diag = jnp.where(rows == cols, jnp.broadcast_to(d_ref[...], (SSM_GROUP, SSM_GROUP)), 0.0)
    kt_ref[q - 1] = ktf[0:SSM_GROUP] + ktb[0:SSM_GROUP] + diag
    for k in range(1, q):
        blk = slice(k * SSM_GROUP, (k + 1) * SSM_GROUP)
        kt_ref[q - 1 + k] = ktf[blk]
        kt_ref[q - 1 - k] = ktb[blk]


def _s5_prep(lam_re, lam_im, log_dt, b_re, b_im, c_re, c_im, d_skip):
    depth = lam_re.shape[0]
    p, c, q = SSM_STATE, SSM_GROUP, S5_CHUNK
    g = depth * SSM_GROUPS

    def flat(x):
        return jnp.swapaxes(x, 0, 1).reshape((2, g) + x.shape[3:])

    lr = flat(lam_re).reshape(2, g, 1, p)
    li = flat(lam_im).reshape(2, g, 1, p)
    ldt = jnp.broadcast_to(flat(log_dt).reshape(2, g, 1, 1), (2, g, 1, p))
    bt_re = flat(jnp.swapaxes(b_re, -1, -2))
    bt_im = flat(jnp.swapaxes(b_im, -1, -2))
    c_re = flat(c_re)
    c_im = flat(c_im)
    d3 = d_skip.reshape(g, 1, c)
    vec_spec = pl.BlockSpec((2, None, 1, p), lambda i: (0, i, 0, 0))
    mat_spec = pl.BlockSpec((2, None, c, p), lambda i: (0, i, 0, 0))
    wt, ct, aq, kt = pl.pallas_call(
        _s5prep_kernel,
        grid=(g,),
        in_specs=[vec_spec, vec_spec, vec_spec, mat_spec, mat_spec, mat_spec, mat_spec,
                  pl.BlockSpec((None, 1, c), lambda i: (i, 0, 0))],
        out_specs=[
            pl.BlockSpec((None, 2, q, c, 2 * p), lambda i: (i, 0, 0, 0, 0)),
            pl.BlockSpec((None, 2, q + 1, c, 2 * p), lambda i: (i, 0, 0, 0, 0)),
            pl.BlockSpec((None, 2, 1, 2 * p), lambda i: (i, 0, 0, 0)),
            pl.BlockSpec((None, 2 * q - 1, c, c), lambda i: (i, 0, 0, 0)),
        ],
        out_shape=[
            jax.ShapeDtypeStruct((g, 2, q, c, 2 * p), F32),
            jax.ShapeDtypeStruct((g, 2, q + 1, c, 2 * p), F32),
            jax.ShapeDtypeStruct((g, 2, 1, 2 * p), F32),
            jax.ShapeDtypeStruct((g, 2 * q - 1, c, c), F32),
        ],
        compiler_params=_params(("parallel",)),
        name="s5prep",
    )(lr, li, ldt, bt_re, bt_im, c_re, c_im, d3)

    nb, gb = depth * S5_BLOCKS, S5_BLOCK_GROUPS
    same = jnp.eye(gb, dtype=bool)

    def place(x, g_axis, new_axis):
        x = jnp.expand_dims(x, new_axis)
        shape = [1] * x.ndim
        shape[g_axis if g_axis < new_axis else g_axis + 1] = gb
        shape[new_axis] = gb
        return jnp.where(same.reshape(shape), x, 0.0)

    idx = jnp.arange(q)[None, :] - jnp.arange(q)[:, None] + (q - 1)
    k5 = kt[:, idx].reshape(nb, gb, q, q, c, c).transpose(0, 2, 1, 4, 3, 5)
    toep = place(k5, 2, 5).reshape(nb, S5_BLOCK_W, S5_BLOCK_W).astype(BF16)

    def in_op(w):
        w = w.reshape(nb, gb, q, c, 2, p).transpose(0, 2, 1, 3, 4, 5)
        return place(w, 2, 5).reshape(nb, S5_BLOCK_W, 2 * S5_STATE_W).astype(BF16)

    def out_op(w):
        w = w.reshape(nb, gb, q, c, 2, p).transpose(0, 4, 1, 5, 2, 3)
        return place(w, 2, 5).reshape(nb, 2 * S5_STATE_W, S5_BLOCK_W).astype(BF16)

    w1f = in_op(wt[:, 0, ::-1])
    w1b = in_op(wt[:, 1])
    w2f = out_op(ct[:, 0, 1:])
    w2b = out_op(ct[:, 1, 1:][:, ::-1])
    aqb = aq.reshape(nb, gb, 2, 2, p).transpose(0, 2, 3, 1, 4).reshape(nb, 2, 2, S5_STATE_W)
    return toep, w1f, w1b, w2f, w2b, aqb


def _chunk_scan(e_ref, s_ref, aq_ref, *, cb, bp, reverse):
    sw = S5_STATE_W
    a_re = jnp.broadcast_to(aq_ref[0:1, :], (bp, sw))
    a_im = jnp.broadcast_to(aq_ref[1:2, :], (bp, sw))

    def body(i, carry):
        s_re, s_im = carry
        c = (cb - 1 - i) if reverse else i
        rows = pl.ds(pl.multiple_of(c * bp, bp), bp)
        e_re = e_ref[rows, 0:sw]
        e_im = e_ref[rows, sw:2 * sw]
        e_ref[rows, 0:sw] = s_re
        e_ref[rows, sw:2 * sw] = s_im
        return a_re * s_re - a_im * s_im + e_re, a_re * s_im + a_im * s_re + e_im

    s_re, s_im = lax.fori_loop(0, cb, body, (s_ref[:, 0:sw], s_ref[:, sw:2 * sw]))
    s_ref[:, 0:sw] = s_re
    s_ref[:, sw:2 * sw] = s_im


def _s5_bwd_kernel(u_ref, w1_ref, aq_ref, r_ref, e_ref, s_ref, *, cb, bp):
    @pl.when(pl.program_id(1) == 0)
    def _():
        s_ref[...] = jnp.zeros_like(s_ref)

    e_ref[...] = jnp.dot(u_ref[...].astype(BF16), w1_ref[...], preferred_element_type=F32)
    _chunk_scan(e_ref, s_ref, aq_ref, cb=cb, bp=bp, reverse=True)
    r_ref[...] = e_ref[...].astype(BF16)


def _s5_fwd_kernel(u_ref, rin_ref, t_ref, w1_ref, w2f_ref, w2b_ref, aq_ref, y_ref, e_ref, s_ref,
                   *, cb, bp):
    @pl.when(pl.program_id(1) == 0)
    def _():
        s_ref[...] = jnp.zeros_like(s_ref)

    ub = u_ref[...].astype(BF16)
    e_ref[...] = jnp.dot(ub, w1_ref[...], preferred_element_type=F32)
    _chunk_scan(e_ref, s_ref, aq_ref, cb=cb, bp=bp, reverse=False)
    y = jnp.dot(ub, t_ref[...], preferred_element_type=F32)
    y += jnp.dot(e_ref[...].astype(BF16), w2f_ref[...], preferred_element_type=F32)
    y += jnp.dot(rin_ref[...], w2b_ref[...], preferred_element_type=F32)
    y = _gelu_tanh(y)
    for t in range(S5_CHUNK):
        y_ref[pl.ds(t, cb * bp, stride=S5_CHUNK), :] = y[:, t * LANES:(t + 1) * LANES]


def _s5_mix(uc, ops, layer, *, bp):
    toep, w1f, w1b, w2f, w2b, aqb = ops
    m = uc.shape[0]
    rows = S5_ROWS
    cb = rows // bp
    nblk = m // rows
    bw, sw2 = S5_BLOCK_W, 2 * S5_STATE_W
    op0 = layer * S5_BLOCKS

    def wspec(r, c):
        return pl.BlockSpec((None, r, c), lambda g, s: (op0 + g, 0, 0))

    rin = pl.pallas_call(
        functools.partial(_s5_bwd_kernel, cb=cb, bp=bp),
        grid=(S5_BLOCKS, nblk),
        in_specs=[pl.BlockSpec((rows, bw), lambda g, s: (nblk - 1 - s, g)),
                  wspec(bw, sw2),
                  pl.BlockSpec((None, None, 2, S5_STATE_W), lambda g, s: (op0 + g, 1, 0, 0))],
        out_specs=pl.BlockSpec((rows, sw2), lambda g, s: (nblk - 1 - s, g)),
        out_shape=jax.ShapeDtypeStruct((m, S5_BLOCKS * sw2), BF16),
        scratch_shapes=[pltpu.VMEM((rows, sw2), F32), pltpu.VMEM((bp, sw2), F32)],
        compiler_params=_params(("parallel", "arbitrary")),
        name="s5bwd",
    )(uc, w1b, aqb)

    return pl.pallas_call(
        functools.partial(_s5_fwd_kernel, cb=cb, bp=bp),
        grid=(S5_BLOCKS, nblk),
        in_specs=[pl.BlockSpec((rows, bw), lambda g, s: (s, g)),
                  pl.BlockSpec((rows, sw2), lambda g, s: (s, g)),
                  wspec(bw, bw), wspec(bw, sw2), wspec(sw2, bw), wspec(sw2, bw),
                  pl.BlockSpec((None, None, 2, S5_STATE_W), lambda g, s: (op0 + g, 0, 0, 0))],
        out_specs=pl.BlockSpec((rows * S5_CHUNK, LANES), lambda g, s: (s, g)),
        out_shape=jax.ShapeDtypeStruct((m * S5_CHUNK, SSM_CH), F32),
        scratch_shapes=[pltpu.VMEM((rows, sw2), F32), pltpu.VMEM((bp, sw2), F32)],
        compiler_params=_params(("parallel", "arbitrary")),
        name="s5fwd",
    )(uc, rin, toep, w1f, w2f, w2b, aqb)


def _attend(q, k, v, mask, sink):
    s = lax.dot_general(q, k, (((1,), (1,)), ((), ())), preferred_element_type=F32)
    s = jnp.where(mask, s * (HEAD_DIM ** -0.5), NEG_INF)
    mx = jnp.max(s, axis=-1, keepdims=True)
    if sink is not None:
        mx = jnp.maximum(mx, sink)
    p = jnp.exp(s - mx)
    den = jnp.sum(p, axis=-1, keepdims=True)
    if sink is not None:
        den = den + jnp.exp(sink - mx)
    o = jnp.dot(p.astype(BF16), v, preferred_element_type=F32) / den
    return o, mx + jnp.log(den)


def _band_window(n, tq, half):
    w = min(tq + 2 * half, n)
    align = math.gcd(math.gcd(tq, half), n - w) if n > w else tq
    return w, align


def _swa_kernel(sink_ref, q_ref, k_ref, v_ref, o_ref, *, n, tq, heads, layer):
    half = SWA_HALF
    w, align = _band_window(n, tq, half)
    sinks = [sink_ref[layer, pl.program_id(1) * heads + g] for g in range(heads)]
    col = lax.broadcasted_iota(jnp.int32, (tq, w), 1)
    row = lax.broadcasted_iota(jnp.int32, (tq, w), 0)

    def body(i, _):
        q0 = pl.multiple_of(i * tq, tq)
        start = pl.multiple_of(jnp.clip(q0 - half, 0, n - w), align)
        k = k_ref[pl.ds(start, w), :]
        v = v_ref[pl.ds(start, w), :]
        mask = jnp.abs(col - row + (start - q0)) <= half
        for g in range(heads):
            cols = slice(g * HEAD_DIM, (g + 1) * HEAD_DIM)
            o, _ = _attend(q_ref[pl.ds(q0, tq), cols], k, v, mask, sinks[g])
            o_ref[pl.ds(q0, tq), cols] = o.astype(o_ref.dtype)
        return 0

    lax.fori_loop(0, n // tq, body, 0, unroll=ATTN_UNROLL)


def _swa(z3, sink, layer, *, tq=128):
    bsz, n, _ = z3.shape
    g = SWA_Q_HEADS // SWA_KV_HEADS
    return pl.pallas_call(
        functools.partial(_swa_kernel, n=n, tq=tq, heads=g, layer=layer),
        grid=(bsz, SWA_KV_HEADS),
        in_specs=[
            pl.BlockSpec(memory_space=pltpu.SMEM),
            pl.BlockSpec((None, n, g * HEAD_DIM), lambda b, h: (b, 0, BQ_H0 // g + h)),
            pl.BlockSpec((None, n, HEAD_DIM), lambda b, h: (b, 0, BK_H0 + h)),
            pl.BlockSpec((None, n, HEAD_DIM), lambda b, h: (b, 0, BV_H0 + h)),
        ],
        out_specs=pl.BlockSpec((None, n, g * HEAD_DIM), lambda b, h: (b, 0, h)),
        out_shape=jax.ShapeDtypeStruct((bsz, n, BQ_W), BF16),
        compiler_params=_params(("parallel", "parallel")),
        name="swa",
    )(sink, z3, z3, z3)


def _dilated_kernel(*refs, seq_len, tq):
    ng = len(DIL_PAIRS)
    qkv = [refs[3 * g:3 * g + 3] for g in range(ng)]
    o_ref = refs[3 * ng]
    stage_ref, qf_ref, kf_ref, vf_ref, uo_ref, ul_ref = refs[3 * ng + 1:]

    for gi, (wdw, d) in enumerate(DIL_PAIRS):
        half = wdw // (2 * d)
        n = seq_len // d
        w, align = _band_window(n, tq, half)
        if d > 1:
            for src, dst in zip(qkv[gi], (qf_ref, kf_ref, vf_ref)):
                stage_ref[...] = src[...].astype(F32)
                for r in range(d):
                    dst[r * n:(r + 1) * n, :] = stage_ref[pl.ds(r, n, stride=d), :].astype(BF16)
            q_ref, k_ref, v_ref = qf_ref, kf_ref, vf_ref
        else:
            q_ref, k_ref, v_ref = qkv[gi]
        col = lax.broadcasted_iota(jnp.int32, (tq, w), 1)
        row = lax.broadcasted_iota(jnp.int32, (tq, w), 0)
        per_res = n // tq

        def body(blk, _, d=d, n=n, w=w, align=align, half=half, per_res=per_res, gi=gi,
                 q_ref=q_ref, k_ref=k_ref, v_ref=v_ref, col=col, row=row):
            r = blk // per_res
            q0 = pl.multiple_of((blk % per_res) * tq, tq)
            base = pl.multiple_of(r * n, tq)
            start = pl.multiple_of(jnp.clip(q0 - half, 0, n - w), align)
            k = k_ref[pl.ds(base + start, w), :]
            v = v_ref[pl.ds(base + start, w), :]
            mask = jnp.abs(col - row + (start - q0)) <= half
            o, lse = _attend(q_ref[pl.ds(base + q0, tq), :], k, v, mask, None)
            lse = jnp.broadcast_to(lse, (tq, HEAD_DIM))
            if d > 1:
                rows = pl.ds(q0 * d + r, tq, stride=d)
            else:
                rows = pl.ds(q0, tq)
            uo_ref[gi, rows, :] = o
            ul_ref[gi, rows, :] = lse
            return 0

        lax.fori_loop(0, seq_len // tq, body, 0, unroll=ATTN_UNROLL)

    def merge(i, _):
        rows = pl.ds(pl.multiple_of(i * tq, tq), tq)
        ls = [ul_ref[g, rows, :] for g in range(ng)]
        mx = functools.reduce(jnp.maximum, ls)
        es = [jnp.exp(l - mx) for l in ls]
        num = sum(e * uo_ref[g, rows, :] for g, e in enumerate(es))
        o_ref[rows, :] = (num / sum(es)).astype(o_ref.dtype)
        return 0

    lax.fori_loop(0, seq_len // tq, merge, 0)


def _dilated(z3, *, tq=128):
    bsz, seq_len, _ = z3.shape
    hp = DIL_HEADS_PER_GROUP
    ng = len(DIL_PAIRS)

    def head_spec(h0, gi):
        return pl.BlockSpec((None, seq_len, HEAD_DIM), lambda b, h: (b, 0, h0 + gi * hp + h))

    in_specs = [head_spec(h0, gi) for gi in range(ng) for h0 in (CQ_H0, CK_H0, CV_H0)]
    return pl.pallas_call(
        functools.partial(_dilated_kernel, seq_len=seq_len, tq=tq),
        grid=(bsz, hp),
        in_specs=in_specs,
        out_specs=pl.BlockSpec((None, seq_len, HEAD_DIM), lambda b, h: (b, 0, h)),
        out_shape=jax.ShapeDtypeStruct((bsz, seq_len, YC_W), BF16),
        scratch_shapes=[pltpu.VMEM((seq_len, HEAD_DIM), F32),
                        pltpu.VMEM((seq_len, HEAD_DIM), BF16),
                        pltpu.VMEM((seq_len, HEAD_DIM), BF16),
                        pltpu.VMEM((seq_len, HEAD_DIM), BF16),
                        pltpu.VMEM((ng, seq_len, HEAD_DIM), F32),
                        pltpu.VMEM((ng, seq_len, HEAD_DIM), F32)],
        compiler_params=_params(("parallel", "parallel")),
        name="dilated",
    )(*([z3] * (3 * ng)))


def _outproj_kernel(ya_ref, gw_ref, gb_ref, yb_ref, yc_ref, w_ref, g_ref, x_ref, out_ref, ya_scr,
                    *, bsz, tq, bp):
    rows = bsz * tq
    q = S5_CHUNK
    chunks = tq // q
    for c in range(chunks):
        for b in range(bsz):
            ya_scr[b * tq + c * q:b * tq + (c + 1) * q, :] = ya_ref[(c * bp + b) * q:(c * bp + b + 1) * q, :]
    ya = ya_scr[...]
    gate = jnp.dot(ya.astype(BF16), gw_ref[...], preferred_element_type=F32) + gb_ref[...]
    a = ya * (1.0 / (1.0 + jnp.exp(-gate)))
    acc = jnp.dot(a.astype(BF16), w_ref[0:A_W, :], preferred_element_type=F32)
    acc += jnp.dot(yb_ref[...].reshape(rows, BQ_W), w_ref[A_W:A_W + BQ_W, :],
                   preferred_element_type=F32)
    acc += jnp.dot(yc_ref[...].reshape(rows, YC_W), w_ref[A_W + BQ_W:OUT_ROWS, :],
                   preferred_element_type=F32)
    x = x_ref[...].reshape(rows, D_MODEL)
    out_ref[...] = (x + _rms(acc, g_ref[...])).reshape(bsz, tq, D_MODEL)


def _outproj(ya, glu_w, glu_b, yb, yc, w_out, g_post, layer, x3):
    bsz, seq_len, _ = x3.shape
    tq = TOKEN_TILE // bsz
    bp = _padded_batch(bsz)

    def tile(width):
        return pl.BlockSpec((bsz, tq, width), lambda i: (0, i, 0))

    def whole(shape):
        return pl.BlockSpec((None,) + shape, lambda i: (layer, 0, 0))

    return pl.pallas_call(
        functools.partial(_outproj_kernel, bsz=bsz, tq=tq, bp=bp),
        grid=(seq_len // tq,),
        in_specs=[pl.BlockSpec((tq * bp, A_W), lambda i: (i, 0)),
                  whole((A_W, A_W)), whole((1, A_W)), tile(BQ_W), tile(YC_W),
                  whole((OUT_ROWS, D_MODEL)), whole((1, D_MODEL)), tile(D_MODEL)],
        out_specs=tile(D_MODEL),
        out_shape=jax.ShapeDtypeStruct((bsz, seq_len, D_MODEL), F32),
        scratch_shapes=[pltpu.VMEM((bsz * tq, A_W), F32)],
        compiler_params=_params(("parallel",)),
        name="outproj",
    )(ya, glu_w, glu_b, yb, yc, w_out, g_post, x3)


def _ffn_kernel(xp_ref, x_ref, xq_ref, gpre_ref, wg_ref, wu_ref, cw_ref, cb_ref, wd_ref, gpost_ref,
                o_ref, xn_ref, g_ref, *, tm, tiles_per_seq):
    i = pl.program_id(0)
    j = pl.program_id(1)
    hb = BF16_ROWS

    @pl.when(j == 0)
    def _():
        gpre = gpre_ref[...]
        xn_ref[0:hb, :] = _rms(xp_ref[...], gpre).astype(BF16)
        xn_ref[hb:hb + tm, :] = _rms(x_ref[...], gpre).astype(BF16)
        xn_ref[hb + tm:2 * hb + tm, :] = _rms(xq_ref[...], gpre).astype(BF16)
        o_ref[...] = jnp.zeros_like(o_ref)

    g_ref[...] = jnp.dot(xn_ref[...], wg_ref[...], preferred_element_type=F32)
    up = jnp.dot(xn_ref[hb:hb + tm, :], wu_ref[...], preferred_element_type=F32)
    pos = i % tiles_per_seq
    row = lax.broadcasted_iota(jnp.int32, (tm, 1), 0)
    keep_prev = jnp.logical_or(row > 0, pos > 0)
    keep_next = jnp.logical_or(row < tm - 1, pos < tiles_per_seq - 1)
    g_prev = jnp.where(keep_prev, g_ref[hb - 1:hb - 1 + tm, :], 0.0)
    g_next = jnp.where(keep_next, g_ref[hb + 1:hb + 1 + tm, :], 0.0)
    g = (g_prev * cw_ref[0:1, :] + g_ref[hb:hb + tm, :] * cw_ref[1:2, :] + g_next * cw_ref[2:3, :]
         + cb_ref[...])
    h = _gelu_tanh(g) * up
    o_ref[...] += jnp.dot(h.astype(BF16), wd_ref[...], preferred_element_type=F32)

    @pl.when(j == pl.num_programs(1) - 1)
    def _():
        o_ref[...] = x_ref[...] + _rms(o_ref[...], gpost_ref[...])


def _ffn(x, g_pre, wg, wu, cw, cb, wd, g_post, layer, seq_len, *, tm=TOKEN_TILE, tf=1024):
    t = x.shape[0]
    hb = BF16_ROWS
    tiles_per_seq = seq_len // tm
    halo_blocks = t // hb
    r = tm // hb
    return pl.pallas_call(
        functools.partial(_ffn_kernel, tm=tm, tiles_per_seq=tiles_per_seq),
        grid=(t // tm, D_FF_PAD // tf),
        in_specs=[
            pl.BlockSpec((hb, D_MODEL), lambda i, j: (jnp.maximum(i * r - 1, 0), 0)),
            pl.BlockSpec((tm, D_MODEL), lambda i, j: (i, 0)),
            pl.BlockSpec((hb, D_MODEL), lambda i, j: (jnp.minimum((i + 1) * r, halo_blocks - 1), 0)),
            pl.BlockSpec((None, 1, D_MODEL), lambda i, j: (layer, 0, 0)),
            pl.BlockSpec((None, D_MODEL, tf), lambda i, j: (layer, 0, j)),
            pl.BlockSpec((None, D_MODEL, tf), lambda i, j: (layer, 0, j)),
            pl.BlockSpec((None, 3, tf), lambda i, j: (layer, 0, j)),
            pl.BlockSpec((None, 1, tf), lambda i, j: (layer, 0, j)),
            pl.BlockSpec((None, tf, D_MODEL), lambda i, j: (layer, j, 0)),
            pl.BlockSpec((None, 1, D_MODEL), lambda i, j: (layer, 0, 0)),
        ],
        out_specs=pl.BlockSpec((tm, D_MODEL), lambda i, j: (i, 0)),
        out_shape=jax.ShapeDtypeStruct((t, D_MODEL), F32),
        scratch_shapes=[pltpu.VMEM((tm + 2 * hb, D_MODEL), BF16),
                        pltpu.VMEM((tm + 2 * hb, tf), F32)],
        compiler_params=_params(("parallel", "arbitrary")),
        name="ffn",
    )(x, x, x, g_pre, wg, wu, cw, cb, wd, g_post)


def _prepare(ln_mix_pre, ln_mix_post, w_in, ssm_lam_re, ssm_lam_im, ssm_log_dt, ssm_b_re, ssm_b_im,
             ssm_c_re, ssm_c_im, ssm_d, ssm_glu_w, ssm_glu_b, swa_sink, w_out, ln_ffn_pre, ln_ffn_post,
             ffn_w_gate, ffn_w_up, ffn_conv_w, ffn_conv_b, ffn_w_down):
    depth = w_in.shape[0]
    fpad = D_FF_PAD - D_FF

    def row(v):
        return v.reshape(depth, 1, v.shape[-1])

    return dict(
        ln_mix_pre=row(ln_mix_pre), ln_mix_post=row(ln_mix_post),
        w_in=w_in.astype(BF16),
        s5=_s5_prep(ssm_lam_re, ssm_lam_im, ssm_log_dt, ssm_b_re, ssm_b_im, ssm_c_re, ssm_c_im, ssm_d),
        glu_w=ssm_glu_w.astype(BF16), glu_b=row(ssm_glu_b),
        sink=swa_sink,
        w_out=w_out.astype(BF16),
        ln_ffn_pre=row(ln_ffn_pre), ln_ffn_post=row(ln_ffn_post),
        wg=jnp.pad(ffn_w_gate.astype(BF16), ((0, 0), (0, 0), (0, fpad))),
        wu=jnp.pad(ffn_w_up.astype(BF16), ((0, 0), (0, 0), (0, fpad))),
        cw=jnp.pad(ffn_conv_w, ((0, 0), (0, 0), (0, fpad))),
        cb=jnp.pad(row(ffn_conv_b), ((0, 0), (0, 0), (0, fpad))),
        wd=jnp.pad(ffn_w_down.astype(BF16), ((0, 0), (0, fpad), (0, 0))),
    )


def _layer(x3, wts, layer, tabs):
    bsz, seq_len, _ = x3.shape
    z3, uc = _inproj(x3, wts["ln_mix_pre"], wts["w_in"], layer, tabs)
    ya = _s5_mix(uc, wts["s5"], layer, bp=_padded_batch(bsz))
    yb = _swa(z3, wts["sink"], layer)
    yc = _dilated(z3)
    x3 = _outproj(ya, wts["glu_w"], wts["glu_b"], yb, yc, wts["w_out"], wts["ln_mix_post"], layer, x3)
    x = _ffn(x3.reshape(bsz * seq_len, D_MODEL), wts["ln_ffn_pre"], wts["wg"], wts["wu"], wts["cw"],
             wts["cb"], wts["wd"], wts["ln_ffn_post"], layer, seq_len)
    return x.reshape(bsz, seq_len, D_MODEL)


def kernel(x_prompt, x_sample, ln_mix_pre, ln_mix_post, w_in, ssm_lam_re, ssm_lam_im, ssm_log_dt, ssm_b_re, ssm_b_im, ssm_c_re, ssm_c_im, ssm_d, ssm_glu_w, ssm_glu_b, swa_sink, w_out, ln_ffn_pre, ln_ffn_post, ffn_w_gate, ffn_w_up, ffn_conv_w, ffn_conv_b, ffn_w_down):
    wts = _prepare(ln_mix_pre, ln_mix_post, w_in, ssm_lam_re, ssm_lam_im, ssm_log_dt, ssm_b_re, ssm_b_im,
                   ssm_c_re, ssm_c_im, ssm_d, ssm_glu_w, ssm_glu_b, swa_sink, w_out, ln_ffn_pre,
                   ln_ffn_post, ffn_w_gate, ffn_w_up, ffn_conv_w, ffn_conv_b, ffn_w_down)
    outs = []
    for x3 in (x_prompt, x_sample):
        tabs = _rope_tables(x3.shape[1])
        for layer in range(w_in.shape[0]):
            x3 = _layer(x3, wts, layer, tabs)
        outs.append(x3)
    return tuple(outs)
```

```python
import functools
import math

import jax
import jax.numpy as jnp
from jax import lax
from jax.experimental import pallas as pl
from jax.experimental.pallas import tpu as pltpu

F32 = jnp.float32
BF16 = jnp.bfloat16

D_MODEL = 2048
HEAD_DIM = 128
SSM_CH = 768
SSM_GROUP = 16
SSM_GROUPS = SSM_CH // SSM_GROUP
SSM_STATE = 64
SWA_Q_HEADS = 4
SWA_KV_HEADS = 2
SWA_HALF = 128
DIL_PAIRS = ((128, 1), (512, 4), (2048, 16))
DIL_HEADS_PER_GROUP = 2
DIL_HEADS = DIL_HEADS_PER_GROUP * len(DIL_PAIRS)
ROPE_THETA = 500000.0
ROPE_DIM = HEAD_DIM // 4
D_FF = 7040
NORM_EPS = 1e-6
NEG_INF = -1e30

A_W = SSM_CH
BQ_W = SWA_Q_HEADS * HEAD_DIM
BKV_W = SWA_KV_HEADS * HEAD_DIM
C_W = DIL_HEADS * HEAD_DIM
YC_W = DIL_HEADS_PER_GROUP * HEAD_DIM
IN_COLS = A_W + BQ_W + 2 * BKV_W + 3 * C_W
OUT_ROWS = A_W + BQ_W + YC_W
BQ_H0 = A_W // HEAD_DIM
BK_H0 = BQ_H0 + SWA_Q_HEADS
BV_H0 = BK_H0 + SWA_KV_HEADS
CQ_H0 = BV_H0 + SWA_KV_HEADS
CK_H0 = CQ_H0 + DIL_HEADS
CV_H0 = CK_H0 + DIL_HEADS

LANES = 128
SUBLANES = 8
BF16_ROWS = 16
MXU_DIM = 256
VMEM_LIMIT = 56 << 20

TOKEN_TILE = 512
S5_CHUNK = SUBLANES
S5_BLOCK_GROUPS = LANES // SSM_GROUP
S5_BLOCKS = SSM_GROUPS // S5_BLOCK_GROUPS
S5_BLOCK_W = S5_CHUNK * LANES
S5_STATE_W = S5_BLOCK_GROUPS * SSM_STATE
S5_ROWS = 512
ATTN_UNROLL = 4
D_FF_PAD = -(-D_FF // (2 * MXU_DIM)) * (2 * MXU_DIM)


def _params(sem):
    return pltpu.CompilerParams(dimension_semantics=sem, vmem_limit_bytes=VMEM_LIMIT)


def _rms(x, g):
    ms = jnp.mean(x * x, axis=-1, keepdims=True)
    return x * lax.rsqrt(ms + NORM_EPS) * g


def _gelu_tanh(x):
    c = math.sqrt(2.0 / math.pi)
    return 0.5 * x * (1.0 + jnp.tanh(c * (x + 0.044715 * (x * x * x))))


def _padded_batch(bsz):
    return -(-bsz // SUBLANES) * SUBLANES


def _is_rope_head(h):
    return BQ_H0 <= h < BV_H0 or CQ_H0 <= h < CV_H0


def _inproj_kernel(x_ref, g_ref, w_ref, cos_ref, sa_ref, sb_ref, o_ref, u_ref, xn_ref, piece_ref,
                   *, bsz, tq, bp, tn):
    rows = bsz * tq
    q = S5_CHUNK
    chunks = tq // q
    heads_per_tile = tn // HEAD_DIM
    u_heads = A_W // HEAD_DIM
    xn_ref[...] = _rms(x_ref[...].reshape(rows, D_MODEL), g_ref[...]).astype(BF16)

    def per_seq(v):
        return v.reshape(bsz, tq, HEAD_DIM)

    for jt in range(IN_COLS // tn):
        acc = jnp.dot(xn_ref[...], w_ref[:, jt * tn:(jt + 1) * tn], preferred_element_type=F32)
        for h in range(heads_per_tile):
            head = jt * heads_per_tile + h
            cols = slice(head * HEAD_DIM, (head + 1) * HEAD_DIM)
            zh = acc[:, h * HEAD_DIM:(h + 1) * HEAD_DIM]
            if _is_rope_head(head):
                r = (per_seq(zh) * cos_ref[...][None]
                     + per_seq(pltpu.roll(zh, HEAD_DIM - ROPE_DIM // 2, 1)) * sa_ref[...][None]
                     + per_seq(pltpu.roll(zh, ROPE_DIM // 2, 1)) * sb_ref[...][None])
                o_ref[:, :, cols] = r.astype(BF16)
            else:
                o_ref[:, :, cols] = per_seq(zh).astype(BF16)
            if head < u_heads:
                if bp > bsz:
                    piece_ref[head] = jnp.zeros(piece_ref.shape[1:], F32)
                for c in range(chunks):
                    for b in range(bsz):
                        piece_ref[head, (c * bp + b) * q:(c * bp + b + 1) * q, :] = (
                            zh[b * tq + c * q:b * tq + (c + 1) * q, :])
                for t in range(q):
                    u_ref[:, head * S5_BLOCK_W + t * LANES:head * S5_BLOCK_W + (t + 1) * LANES] = (
                        piece_ref[head, pl.ds(t, chunks * bp, stride=q), :])


def _inproj(x3, g_all, w_all, layer, tabs, *, tn=512):
    bsz, seq_len, _ = x3.shape
    tq = TOKEN_TILE // bsz
    bp = _padded_batch(bsz)
    chunks = tq // S5_CHUNK
    cos_t, sa_t, sb_t = tabs
    tab_spec = pl.BlockSpec((tq, HEAD_DIM), lambda i: (i, 0))
    return pl.pallas_call(
        functools.partial(_inproj_kernel, bsz=bsz, tq=tq, bp=bp, tn=tn),
        grid=(seq_len // tq,),
        in_specs=[
            pl.BlockSpec((bsz, tq, D_MODEL), lambda i: (0, i, 0)),
            pl.BlockSpec((None, 1, D_MODEL), lambda i: (layer, 0, 0)),
            pl.BlockSpec((None, D_MODEL, IN_COLS), lambda i: (layer, 0, 0),
                         pipeline_mode=pl.Buffered(1)),
            tab_spec, tab_spec, tab_spec,
        ],
        out_specs=[
            pl.BlockSpec((bsz, tq, IN_COLS), lambda i: (0, i, 0)),
            pl.BlockSpec((chunks * bp, S5_BLOCKS * S5_BLOCK_W), lambda i: (i, 0)),
        ],
        out_shape=[
            jax.ShapeDtypeStruct((bsz, seq_len, IN_COLS), BF16),
            jax.ShapeDtypeStruct((seq_len // S5_CHUNK * bp, S5_BLOCKS * S5_BLOCK_W), F32),
        ],
        scratch_shapes=[pltpu.VMEM((bsz * tq, D_MODEL), BF16),
                        pltpu.VMEM((A_W // HEAD_DIM, chunks * bp * S5_CHUNK, LANES), F32)],
        compiler_params=_params(("parallel",)),
        name="inproj",
    )(x3, g_all, w_all, cos_t, sa_t, sb_t)


def _rope_tables(seq_len):
    half = ROPE_DIM // 2
    inv = ROPE_THETA ** (-jnp.arange(0, ROPE_DIM, 2, dtype=F32) / ROPE_DIM)
    ang = jnp.arange(seq_len, dtype=F32)[:, None] * inv[None, :]
    cos, sin = jnp.cos(ang), jnp.sin(ang)
    zeros = jnp.zeros((seq_len, HEAD_DIM - ROPE_DIM), F32)
    zh = jnp.zeros((seq_len, half), F32)
    cos_t = jnp.concatenate([cos, cos, jnp.ones((seq_len, HEAD_DIM - ROPE_DIM), F32)], axis=1)
    sa_t = jnp.concatenate([-sin, zh, zeros], axis=1)
    sb_t = jnp.concatenate([zh, sin, zeros], axis=1)
    return cos_t, sa_t, sb_t


def _s5prep_kernel(lr_ref, li_ref, ldt_ref, bre_ref, bim_ref, cre_ref, cim_ref, d_ref,
                   toep_ref, w1f_ref, w1b_ref, w2f_ref, w2b_ref, aq_ref, wt_scr, ct_scr, kt_scr):
    q, p, c, gb = S5_CHUNK, SSM_STATE, SSM_GROUP, S5_BLOCK_GROUPS
    sw = S5_STATE_W
    c_bits, p_bits = c.bit_length() - 1, p.bit_length() - 1
    nt = (((1,), (1,)), ((), ()))
    rows16 = lax.broadcasted_iota(jnp.int32, (c, c), 0)
    cols16 = lax.broadcasted_iota(jnp.int32, (c, c), 1)
    for gl in range(gb):
        grp = slice(gl * c, (gl + 1) * c)
        kts = []
        for r in range(2):
            lr = lr_ref[r, gl]
            li = li_ref[r, gl]
            dt = jnp.exp(ldt_ref[r, gl])
            mag = jnp.exp(lr * dt)
            ab_re = mag * jnp.cos(li * dt)
            ab_im = mag * jnp.sin(li * dt)
            nr = ab_re - 1.0
            den = lr * lr + li * li
            z_re = (nr * lr + ab_im * li) / den
            z_im = (ab_im * lr - nr * li) / den
            b_re = bre_ref[r, gl]
            b_im = bim_ref[r, gl]
            bb_re = z_re * b_re - z_im * b_im
            bb_im = z_re * b_im + z_im * b_re
            c_re = cre_ref[r, gl]
            c_im = cim_ref[r, gl]
            pw_re = jnp.ones((1, p), F32)
            pw_im = jnp.zeros((1, p), F32)
            for k in range(q + 1):
                if k < q:
                    wt_scr[r, k, grp, 0:p] = pw_re * bb_re - pw_im * bb_im
                    wt_scr[r, k, grp, p:2 * p] = pw_re * bb_im + pw_im * bb_re
                ct_scr[r, k, grp, 0:p] = c_re * pw_re - c_im * pw_im
                ct_scr[r, k, grp, p:2 * p] = -(c_re * pw_im + c_im * pw_re)
                if k < q:
                    pw_re, pw_im = pw_re * ab_re - pw_im * ab_im, pw_re * ab_im + pw_im * ab_re
            aq_ref[r, 0:1, gl * p:(gl + 1) * p] = pw_re
            aq_ref[r, 1:2, gl * p:(gl + 1) * p] = pw_im
            wt_flat = jnp.concatenate([wt_scr[r, k, grp, :] for k in range(q)], axis=0)
            kts.append(lax.dot_general(wt_flat, ct_scr[r, 0, grp, :], nt,
                                       precision=lax.Precision.HIGHEST,
                                       preferred_element_type=F32))
        ktf, ktb = kts
        diag = jnp.where(rows16 == cols16, jnp.broadcast_to(d_ref[gl], (c, c)), 0.0)
        kt_scr[q - 1, grp, :] = ktf[0:c] + ktb[0:c] + diag
        for k in range(1, q):
            blk = slice(k * c, (k + 1) * c)
            kt_scr[q - 1 + k, grp, :] = ktf[blk]
            kt_scr[q - 1 - k, grp, :] = ktb[blk]

    def iota(shape, axis):
        return lax.broadcasted_iota(jnp.int32, shape, axis)

    def spread(x, sel):
        return jnp.dot(x, sel, precision=lax.Precision.HIGHEST, preferred_element_type=F32)

    in_mask = iota((LANES, sw), 0) >> c_bits == iota((LANES, sw), 1) >> p_bits
    in_sel = [(iota((LANES, sw), 0) == ri * p + (iota((LANES, sw), 1) & (p - 1))).astype(F32)
              for ri in range(2)]
    for j in range(q):
        for ref, src in ((w1f_ref, wt_scr[0, q - 1 - j]), (w1b_ref, wt_scr[1, j])):
            for ri in range(2):
                ref[j * LANES:(j + 1) * LANES, ri * sw:(ri + 1) * sw] = (
                    jnp.where(in_mask, spread(src, in_sel[ri]), 0.0).astype(BF16))

    out_mask = iota((sw, LANES), 0) >> p_bits == iota((sw, LANES), 1) >> c_bits
    for t in range(q):
        for ref, src in ((w2f_ref, ct_scr[0, t + 1]), (w2b_ref, ct_scr[1, q - t])):
            src_t = src.T
            for ri in range(2):
                tiled = jnp.concatenate([src_t[ri * p:(ri + 1) * p, :]] * gb, axis=0)
                ref[ri * sw:(ri + 1) * sw, t * LANES:(t + 1) * LANES] = (
                    jnp.where(out_mask, tiled, 0.0).astype(BF16))

    lag_mask = iota((LANES, LANES), 0) >> c_bits == iota((LANES, LANES), 1) >> c_bits
    lag_sel = (iota((c, LANES), 0) == (iota((c, LANES), 1) & (c - 1))).astype(F32)
    lags = [jnp.where(lag_mask, spread(kt_scr[k], lag_sel), 0.0).astype(BF16) for k in range(2 * q - 1)]
    for j in range(q):
        for t in range(q):
            toep_ref[j * LANES:(j + 1) * LANES, t * LANES:(t + 1) * LANES] = lags[t - j + q - 1]


def _s5_prep(lam_re, lam_im, log_dt, b_re, b_im, c_re, c_im, d_skip):
    depth = lam_re.shape[0]
    p, c, q, gb = SSM_STATE, SSM_GROUP, S5_CHUNK, S5_BLOCK_GROUPS
    g = depth * SSM_GROUPS
    nb = g // gb
    bw, sw2 = S5_BLOCK_W, 2 * S5_STATE_W

    def flat(x):
        return jnp.swapaxes(x, 0, 1).reshape((2, g) + x.shape[3:])

    lr = flat(lam_re).reshape(2, g, 1, p)
    li = flat(lam_im).reshape(2, g, 1, p)
    ldt = jnp.broadcast_to(flat(log_dt).reshape(2, g, 1, 1), (2, g, 1, p))
    bt_re = flat(jnp.swapaxes(b_re, -1, -2))
    bt_im = flat(jnp.swapaxes(b_im, -1, -2))
    d3 = d_skip.reshape(g, 1, c)
    vec_spec = pl.BlockSpec((2, gb, 1, p), lambda i: (0, i, 0, 0))
    mat_spec = pl.BlockSpec((2, gb, c, p), lambda i: (0, i, 0, 0))

    def op_spec(r, w):
        return pl.BlockSpec((None, r, w), lambda i: (i, 0, 0))

    def op_shape(r, w):
        return jax.ShapeDtypeStruct((nb, r, w), BF16)

    return pl.pallas_call(
        _s5prep_kernel,
        grid=(nb,),
        in_specs=[vec_spec, vec_spec, vec_spec, mat_spec, mat_spec, mat_spec, mat_spec,
                  pl.BlockSpec((gb, 1, c), lambda i: (i, 0, 0))],
        out_specs=[op_spec(bw, bw), op_spec(bw, sw2), op_spec(bw, sw2), op_spec(sw2, bw),
                   op_spec(sw2, bw), pl.BlockSpec((None, 2, 2, S5_STATE_W), lambda i: (i, 0, 0, 0))],
        out_shape=[op_shape(bw, bw), op_shape(bw, sw2), op_shape(bw, sw2), op_shape(sw2, bw),
                   op_shape(sw2, bw), jax.ShapeDtypeStruct((nb, 2, 2, S5_STATE_W), F32)],
        scratch_shapes=[pltpu.VMEM((2, q, LANES, 2 * p), F32),
                        pltpu.VMEM((2, q + 1, LANES, 2 * p), F32),
                        pltpu.VMEM((2 * q - 1, LANES, c), F32)],
        compiler_params=_params(("parallel",)),
        name="s5prep",
    )(lr, li, ldt, bt_re, bt_im, flat(c_re), flat(c_im), d3)


def _chunk_scan(e_ref, s_ref, aq_ref, *, cb, bp, reverse):
    sw = S5_STATE_W
    a_re = jnp.broadcast_to(aq_ref[0:1, :], (bp, sw))
    a_im = jnp.broadcast_to(aq_ref[1:2, :], (bp, sw))

    def body(i, carry):
        s_re, s_im = carry
        c = (cb - 1 - i) if reverse else i
        rows = pl.ds(pl.multiple_of(c * bp, bp), bp)
        e_re = e_ref[rows, 0:sw]
        e_im = e_ref[rows, sw:2 * sw]
        e_ref[rows, 0:sw] = s_re
        e_ref[rows, sw:2 * sw] = s_im
        return a_re * s_re - a_im * s_im + e_re, a_re * s_im + a_im * s_re + e_im

    s_re, s_im = lax.fori_loop(0, cb, body, (s_ref[:, 0:sw], s_ref[:, sw:2 * sw]))
    s_ref[:, 0:sw] = s_re
    s_ref[:, sw:2 * sw] = s_im


def _s5_bwd_kernel(u_ref, w1_ref, aq_ref, r_ref, e_ref, s_ref, *, cb, bp):
    @pl.when(pl.program_id(1) == 0)
    def _():
        s_ref[...] = jnp.zeros_like(s_ref)

    e_ref[...] = jnp.dot(u_ref[...].astype(BF16), w1_ref[...], preferred_element_type=F32)
    _chunk_scan(e_ref, s_ref, aq_ref, cb=cb, bp=bp, reverse=True)
    r_ref[...] = e_ref[...].astype(BF16)


def _s5_fwd_kernel(u_ref, rin_ref, t_ref, w1_ref, w2f_ref, w2b_ref, aq_ref, y_ref, e_ref, s_ref,
                   *, cb, bp):
    @pl.when(pl.program_id(1) == 0)
    def _():
        s_ref[...] = jnp.zeros_like(s_ref)

    ub = u_ref[...].astype(BF16)
    e_ref[...] = jnp.dot(ub, w1_ref[...], preferred_element_type=F32)
    _chunk_scan(e_ref, s_ref, aq_ref, cb=cb, bp=bp, reverse=False)
    y = jnp.dot(ub, t_ref[...], preferred_element_type=F32)
    y += jnp.dot(e_ref[...].astype(BF16), w2f_ref[...], preferred_element_type=F32)
    y += jnp.dot(rin_ref[...], w2b_ref[...], preferred_element_type=F32)
    y = _gelu_tanh(y)
    for t in range(S5_CHUNK):
        y_ref[pl.ds(t, cb * bp, stride=S5_CHUNK), :] = y[:, t * LANES:(t + 1) * LANES]


def _s5_mix(uc, ops, layer, *, bp):
    toep, w1f, w1b, w2f, w2b, aqb = ops
    m = uc.shape[0]
    rows = S5_ROWS
    cb = rows // bp
    nblk = m // rows
    bw, sw2 = S5_BLOCK_W, 2 * S5_STATE_W
    op0 = layer * S5_BLOCKS

    def wspec(r, c):
        return pl.BlockSpec((None, r, c), lambda g, s: (op0 + g, 0, 0))

    rin = pl.pallas_call(
        functools.partial(_s5_bwd_kernel, cb=cb, bp=bp),
        grid=(S5_BLOCKS, nblk),
        in_specs=[pl.BlockSpec((rows, bw), lambda g, s: (nblk - 1 - s, g)),
                  wspec(bw, sw2),
                  pl.BlockSpec((None, None, 2, S5_STATE_W), lambda g, s: (op0 + g, 1, 0, 0))],
        out_specs=pl.BlockSpec((rows, sw2), lambda g, s: (nblk - 1 - s, g)),
        out_shape=jax.ShapeDtypeStruct((m, S5_BLOCKS * sw2), BF16),
        scratch_shapes=[pltpu.VMEM((rows, sw2), F32), pltpu.VMEM((bp, sw2), F32)],
        compiler_params=_params(("parallel", "arbitrary")),
        name="s5bwd",
    )(uc, w1b, aqb)

    return pl.pallas_call(
        functools.partial(_s5_fwd_kernel, cb=cb, bp=bp),
        grid=(S5_BLOCKS, nblk),
        in_specs=[pl.BlockSpec((rows, bw), lambda g, s: (s, g)),
                  pl.BlockSpec((rows, sw2), lambda g, s: (s, g)),
                  wspec(bw, bw), wspec(bw, sw2), wspec(sw2, bw), wspec(sw2, bw),
                  pl.BlockSpec((None, None, 2, S5_STATE_W), lambda g, s: (op0 + g, 0, 0, 0))],
        out_specs=pl.BlockSpec((rows * S5_CHUNK, LANES), lambda g, s: (s, g)),
        out_shape=jax.ShapeDtypeStruct((m * S5_CHUNK, SSM_CH), F32),
        scratch_shapes=[pltpu.VMEM((rows, sw2), F32), pltpu.VMEM((bp, sw2), F32)],
        compiler_params=_params(("parallel", "arbitrary")),
        name="s5fwd",
    )(uc, rin, toep, w1f, w2f, w2b, aqb)


def _attend(q, k, v, mask, sink):
    s = lax.dot_general(q, k, (((1,), (1,)), ((), ())), preferred_element_type=F32)
    s = jnp.where(mask, s * (HEAD_DIM ** -0.5), NEG_INF)
    mx = jnp.max(s, axis=-1, keepdims=True)
    if sink is not None:
        mx = jnp.maximum(mx, sink)
    p = jnp.exp(s - mx)
    den = jnp.sum(p, axis=-1, keepdims=True)
    if sink is not None:
        den = den + jnp.exp(sink - mx)
    o = jnp.dot(p.astype(BF16), v, preferred_element_type=F32) / den
    return o, mx + jnp.log(den)


def _band_window(n, tq, half):
    w = min(tq + 2 * half, n)
    align = math.gcd(math.gcd(tq, half), n - w) if n > w else tq
    return w, align


def _swa_kernel(sink_ref, q_ref, k_ref, v_ref, o_ref, *, n, tq, heads, layer):
    half = SWA_HALF
    w, align = _band_window(n, tq, half)
    col = lax.broadcasted_iota(jnp.int32, (heads * tq, w), 1)
    row = lax.broadcasted_iota(jnp.int32, (heads * tq, w), 0) & (tq - 1)
    head_of_row = lax.broadcasted_iota(jnp.int32, (heads * tq, 1), 0) >> (tq.bit_length() - 1)
    sink = jnp.zeros((heads * tq, 1), F32)
    for g in range(heads):
        sink = jnp.where(head_of_row == g, sink_ref[layer, pl.program_id(1) * heads + g], sink)

    def body(i, _):
        q0 = pl.multiple_of(i * tq, tq)
        start = pl.multiple_of(jnp.clip(q0 - half, 0, n - w), align)
        k = k_ref[pl.ds(start, w), :]
        v = v_ref[pl.ds(start, w), :]
        mask = jnp.abs(col - row + (start - q0)) <= half
        q = jnp.concatenate([q_ref[pl.ds(q0, tq), g * HEAD_DIM:(g + 1) * HEAD_DIM]
                             for g in range(heads)], axis=0)
        o, _ = _attend(q, k, v, mask, sink)
        for g in range(heads):
            o_ref[pl.ds(q0, tq), g * HEAD_DIM:(g + 1) * HEAD_DIM] = (
                o[g * tq:(g + 1) * tq].astype(o_ref.dtype))
        return 0

    lax.fori_loop(0, n // tq, body, 0, unroll=ATTN_UNROLL)


def _swa(z3, sink, layer, *, tq=128):
    bsz, n, _ = z3.shape
    g = SWA_Q_HEADS // SWA_KV_HEADS
    return pl.pallas_call(
        functools.partial(_swa_kernel, n=n, tq=tq, heads=g, layer=layer),
        grid=(bsz, SWA_KV_HEADS),
        in_specs=[
            pl.BlockSpec(memory_space=pltpu.SMEM),
            pl.BlockSpec((None, n, g * HEAD_DIM), lambda b, h: (b, 0, BQ_H0 // g + h)),
            pl.BlockSpec((None, n, HEAD_DIM), lambda b, h: (b, 0, BK_H0 + h)),
            pl.BlockSpec((None, n, HEAD_DIM), lambda b, h: (b, 0, BV_H0 + h)),
        ],
        out_specs=pl.BlockSpec((None, n, g * HEAD_DIM), lambda b, h: (b, 0, h)),
        out_shape=jax.ShapeDtypeStruct((bsz, n, BQ_W), BF16),
        compiler_params=_params(("parallel", "parallel")),
        name="swa",
    )(sink, z3, z3, z3)


def _dilated_kernel(*refs, seq_len, tq):
    ng = len(DIL_PAIRS)
    qkv = [refs[3 * g:3 * g + 3] for g in range(ng)]
    o_ref = refs[3 * ng]
    stage_ref, qf_ref, kf_ref, vf_ref, uo_ref, ul_ref = refs[3 * ng + 1:]

    for gi, (wdw, d) in enumerate(DIL_PAIRS):
        half = wdw // (2 * d)
        n = seq_len // d
        w, align = _band_window(n, tq, half)
        if d > 1:
            for src, dst in zip(qkv[gi], (qf_ref, kf_ref, vf_ref)):
                stage_ref[...] = src[...].astype(F32)
                for r in range(d):
                    dst[r * n:(r + 1) * n, :] = stage_ref[pl.ds(r, n, stride=d), :].astype(BF16)
            q_ref, k_ref, v_ref = qf_ref, kf_ref, vf_ref
        else:
            q_ref, k_ref, v_ref = qkv[gi]
        col = lax.broadcasted_iota(jnp.int32, (tq, w), 1)
        row = lax.broadcasted_iota(jnp.int32, (tq, w), 0)
        per_res = n // tq

        def body(blk, _, d=d, n=n, w=w, align=align, half=half, per_res=per_res, gi=gi,
                 q_ref=q_ref, k_ref=k_ref, v_ref=v_ref, col=col, row=row):
            r = blk // per_res
            q0 = pl.multiple_of((blk % per_res) * tq, tq)
            base = pl.multiple_of(r * n, tq)
            start = pl.multiple_of(jnp.clip(q0 - half, 0, n - w), align)
            k = k_ref[pl.ds(base + start, w), :]
            v = v_ref[pl.ds(base + start, w), :]
            mask = jnp.abs(col - row + (start - q0)) <= half
            o, lse = _attend(q_ref[pl.ds(base + q0, tq), :], k, v, mask, None)
            lse = jnp.broadcast_to(lse, (tq, HEAD_DIM))
            if d > 1:
                rows = pl.ds(q0 * d + r, tq, stride=d)
            else:
                rows = pl.ds(q0, tq)
            uo_ref[gi, rows, :] = o
            ul_ref[gi, rows, :] = lse
            return 0

        lax.fori_loop(0, seq_len // tq, body, 0, unroll=ATTN_UNROLL)

    def merge(i, _):
        rows = pl.ds(pl.multiple_of(i * tq, tq), tq)
        ls = [ul_ref[g, rows, :] for g in range(ng)]
        mx = functools.reduce(jnp.maximum, ls)
        es = [jnp.exp(l - mx) for l in ls]
        num = sum(e * uo_ref[g, rows, :] for g, e in enumerate(es))
        o_ref[rows, :] = (num / sum(es)).astype(o_ref.dtype)
        return 0

    lax.fori_loop(0, seq_len // tq, merge, 0)


def _dilated(z3, *, tq=128):
    bsz, seq_len, _ = z3.shape
    hp = DIL_HEADS_PER_GROUP
    ng = len(DIL_PAIRS)

    def head_spec(h0, gi):
        return pl.BlockSpec((None, seq_len, HEAD_DIM), lambda b, h: (b, 0, h0 + gi * hp + h))

    in_specs = [head_spec(h0, gi) for gi in range(ng) for h0 in (CQ_H0, CK_H0, CV_H0)]
    return pl.pallas_call(
        functools.partial(_dilated_kernel, seq_len=seq_len, tq=tq),
        grid=(bsz, hp),
        in_specs=in_specs,
        out_specs=pl.BlockSpec((None, seq_len, HEAD_DIM), lambda b, h: (b, 0, h)),
        out_shape=jax.ShapeDtypeStruct((bsz, seq_len, YC_W), BF16),
        scratch_shapes=[pltpu.VMEM((seq_len, HEAD_DIM), F32),
                        pltpu.VMEM((seq_len, HEAD_DIM), BF16),
                        pltpu.VMEM((seq_len, HEAD_DIM), BF16),
                        pltpu.VMEM((seq_len, HEAD_DIM), BF16),
                        pltpu.VMEM((ng, seq_len, HEAD_DIM), F32),
                        pltpu.VMEM((ng, seq_len, HEAD_DIM), F32)],
        compiler_params=_params(("parallel", "parallel")),
        name="dilated",
    )(*([z3] * (3 * ng)))


def _outproj_kernel(ya_ref, gw_ref, gb_ref, yb_ref, yc_ref, w_ref, g_ref, x_ref, out_ref, ya_scr,
                    *, bsz, tq, bp):
    rows = bsz * tq
    q = S5_CHUNK
    chunks = tq // q
    for c in range(chunks):
        for b in range(bsz):
            ya_scr[b * tq + c * q:b * tq + (c + 1) * q, :] = ya_ref[(c * bp + b) * q:(c * bp + b + 1) * q, :]
    ya = ya_scr[...]
    gate = jnp.dot(ya.astype(BF16), gw_ref[...], preferred_element_type=F32) + gb_ref[...]
    a = ya * (1.0 / (1.0 + jnp.exp(-gate)))
    acc = jnp.dot(a.astype(BF16), w_ref[0:A_W, :], preferred_element_type=F32)
    acc += jnp.dot(yb_ref[...].reshape(rows, BQ_W), w_ref[A_W:A_W + BQ_W, :],
                   preferred_element_type=F32)
    acc += jnp.dot(yc_ref[...].reshape(rows, YC_W), w_ref[A_W + BQ_W:OUT_ROWS, :],
                   preferred_element_type=F32)
    x = x_ref[...].reshape(rows, D_MODEL)
    out_ref[...] = (x + _rms(acc, g_ref[...])).reshape(bsz, tq, D_MODEL)


def _outproj(ya, glu_w, glu_b, yb, yc, w_out, g_post, layer, x3):
    bsz, seq_len, _ = x3.shape
    tq = TOKEN_TILE // bsz
    bp = _padded_batch(bsz)

    def tile(width):
        return pl.BlockSpec((bsz, tq, width), lambda i: (0, i, 0))

    def whole(shape):
        return pl.BlockSpec((None,) + shape, lambda i: (layer, 0, 0))

    return pl.pallas_call(
        functools.partial(_outproj_kernel, bsz=bsz, tq=tq, bp=bp),
        grid=(seq_len // tq,),
        in_specs=[pl.BlockSpec((tq * bp, A_W), lambda i: (i, 0)),
                  whole((A_W, A_W)), whole((1, A_W)), tile(BQ_W), tile(YC_W),
                  whole((OUT_ROWS, D_MODEL)), whole((1, D_MODEL)), tile(D_MODEL)],
        out_specs=tile(D_MODEL),
        out_shape=jax.ShapeDtypeStruct((bsz, seq_len, D_MODEL), F32),
        scratch_shapes=[pltpu.VMEM((bsz * tq, A_W), F32)],
        compiler_params=_params(("parallel",)),
        name="outproj",
    )(ya, glu_w, glu_b, yb, yc, w_out, g_post, x3)


def _ffn_kernel(xp_ref, x_ref, xq_ref, gpre_ref, wg_ref, wu_ref, cw_ref, cb_ref, wd_ref, gpost_ref,
                o_ref, xn_ref, g_ref, *, tm, tiles_per_seq):
    i = pl.program_id(0)
    j = pl.program_id(1)
    hb = BF16_ROWS

    @pl.when(j == 0)
    def _():
        gpre = gpre_ref[...]
        xn_ref[0:hb, :] = _rms(xp_ref[...], gpre).astype(BF16)
        xn_ref[hb:hb + tm, :] = _rms(x_ref[...], gpre).astype(BF16)
        xn_ref[hb + tm:2 * hb + tm, :] = _rms(xq_ref[...], gpre).astype(BF16)
        o_ref[...] = jnp.zeros_like(o_ref)

    pos = i % tiles_per_seq
    row = lax.broadcasted_iota(jnp.int32, (tm, 1), 0)
    keep_prev = jnp.logical_or(row > 0, pos > 0)
    keep_next = jnp.logical_or(row < tm - 1, pos < tiles_per_seq - 1)
    g_ref[...] = jnp.dot(xn_ref[...], wg_ref[...], preferred_element_type=F32)
    up = jnp.dot(xn_ref[hb:hb + tm, :], wu_ref[...], preferred_element_type=F32)
    g_prev = jnp.where(keep_prev, g_ref[hb - 1:hb - 1 + tm, :], 0.0)
    g_next = jnp.where(keep_next, g_ref[hb + 1:hb + 1 + tm, :], 0.0)
    g = (g_prev * cw_ref[0:1, :] + g_ref[hb:hb + tm, :] * cw_ref[1:2, :] + g_next * cw_ref[2:3, :]
         + cb_ref[...])
    h = _gelu_tanh(g) * up
    o_ref[...] += jnp.dot(h.astype(BF16), wd_ref[...], preferred_element_type=F32)

    @pl.when(j == pl.num_programs(1) - 1)
    def _():
        o_ref[...] = x_ref[...] + _rms(o_ref[...], gpost_ref[...])


def _ffn(x, g_pre, wg, wu, cw, cb, wd, g_post, layer, seq_len, *, tm=TOKEN_TILE, tf=1024):
    t = x.shape[0]
    hb = BF16_ROWS
    tiles_per_seq = seq_len // tm
    halo_blocks = t // hb
    r = tm // hb
    return pl.pallas_call(
        functools.partial(_ffn_kernel, tm=tm, tiles_per_seq=tiles_per_seq),
        grid=(t // tm, D_FF_PAD // tf),
        in_specs=[
            pl.BlockSpec((hb, D_MODEL), lambda i, j: (jnp.maximum(i * r - 1, 0), 0)),
            pl.BlockSpec((tm, D_MODEL), lambda i, j: (i, 0)),
            pl.BlockSpec((hb, D_MODEL), lambda i, j: (jnp.minimum((i + 1) * r, halo_blocks - 1), 0)),
            pl.BlockSpec((None, 1, D_MODEL), lambda i, j: (layer, 0, 0)),
            pl.BlockSpec((None, D_MODEL, tf), lambda i, j: (layer, 0, j)),
            pl.BlockSpec((None, D_MODEL, tf), lambda i, j: (layer, 0, j)),
            pl.BlockSpec((None, 3, tf), lambda i, j: (layer, 0, j)),
            pl.BlockSpec((None, 1, tf), lambda i, j: (layer, 0, j)),
            pl.BlockSpec((None, tf, D_MODEL), lambda i, j: (layer, j, 0)),
            pl.BlockSpec((None, 1, D_MODEL), lambda i, j: (layer, 0, 0)),
        ],
        out_specs=pl.BlockSpec((tm, D_MODEL), lambda i, j: (i, 0)),
        out_shape=jax.ShapeDtypeStruct((t, D_MODEL), F32),
        scratch_shapes=[pltpu.VMEM((tm + 2 * hb, D_MODEL), BF16),
                        pltpu.VMEM((tm + 2 * hb, tf), F32)],
        compiler_params=_params(("parallel", "arbitrary")),
        name="ffn",
    )(x, x, x, g_pre, wg, wu, cw, cb, wd, g_post)


def _prepare(ln_mix_pre, ln_mix_post, w_in, ssm_lam_re, ssm_lam_im, ssm_log_dt, ssm_b_re, ssm_b_im,
             ssm_c_re, ssm_c_im, ssm_d, ssm_glu_w, ssm_glu_b, swa_sink, w_out, ln_ffn_pre, ln_ffn_post,
             ffn_w_gate, ffn_w_up, ffn_conv_w, ffn_conv_b, ffn_w_down):
    depth = w_in.shape[0]
    fpad = D_FF_PAD - D_FF

    def row(v):
        return v.reshape(depth, 1, v.shape[-1])

    return dict(
        ln_mix_pre=row(ln_mix_pre), ln_mix_post=row(ln_mix_post),
        w_in=w_in.astype(BF16),
        s5=_s5_prep(ssm_lam_re, ssm_lam_im, ssm_log_dt, ssm_b_re, ssm_b_im, ssm_c_re, ssm_c_im, ssm_d),
        glu_w=ssm_glu_w.astype(BF16), glu_b=row(ssm_glu_b),
        sink=swa_sink,
        w_out=w_out.astype(BF16),
        ln_ffn_pre=row(ln_ffn_pre), ln_ffn_post=row(ln_ffn_post),
        wg=jnp.pad(ffn_w_gate.astype(BF16), ((0, 0), (0, 0), (0, fpad))),
        wu=jnp.pad(ffn_w_up.astype(BF16), ((0, 0), (0, 0), (0, fpad))),
        cw=jnp.pad(ffn_conv_w, ((0, 0), (0, 0), (0, fpad))),
        cb=jnp.pad(row(ffn_conv_b), ((0, 0), (0, 0), (0, fpad))),
        wd=jnp.pad(ffn_w_down.astype(BF16), ((0, 0), (0, fpad), (0, 0))),
    )


def _layer(x3, wts, layer, tabs):
    bsz, seq_len, _ = x3.shape
    z3, uc = _inproj(x3, wts["ln_mix_pre"], wts["w_in"], layer, tabs)
    ya = _s5_mix(uc, wts["s5"], layer, bp=_padded_batch(bsz))
    yb = _swa(z3, wts["sink"], layer)
    yc = _dilated(z3)
    x3 = _outproj(ya, wts["glu_w"], wts["glu_b"], yb, yc, wts["w_out"], wts["ln_mix_post"], layer, x3)
    x = _ffn(x3.reshape(bsz * seq_len, D_MODEL), wts["ln_ffn_pre"], wts["wg"], wts["wu"], wts["cw"],
             wts["cb"], wts["wd"], wts["ln_ffn_post"], layer, seq_len)
    return x.reshape(bsz, seq_len, D_MODEL)


def kernel(x_prompt, x_sample, ln_mix_pre, ln_mix_post, w_in, ssm_lam_re, ssm_lam_im, ssm_log_dt, ssm_b_re, ssm_b_im, ssm_c_re, ssm_c_im, ssm_d, ssm_glu_w, ssm_glu_b, swa_sink, w_out, ln_ffn_pre, ln_ffn_post, ffn_w_gate, ffn_w_up, ffn_conv_w, ffn_conv_b, ffn_w_down):
    wts = _prepare(ln_mix_pre, ln_mix_post, w_in, ssm_lam_re, ssm_lam_im, ssm_log_dt, ssm_b_re, ssm_b_im,
                   ssm_c_re, ssm_c_im, ssm_d, ssm_glu_w, ssm_glu_b, swa_sink, w_out, ln_ffn_pre,
                   ln_ffn_post, ffn_w_gate, ffn_w_up, ffn_conv_w, ffn_conv_b, ffn_w_down)
    outs = []
    for x3 in (x_prompt, x_sample):
        tabs = _rope_tables(x3.shape[1])
        for layer in range(w_in.shape[0]):
            x3 = _layer(x3, wts, layer, tabs)
        outs.append(x3)
    return tuple(outs)
```

```python
import functools
import math

import jax
import jax.numpy as jnp
from jax import lax
from jax.experimental import pallas as pl
from jax.experimental.pallas import tpu as pltpu

F32 = jnp.float32
BF16 = jnp.bfloat16

D_MODEL = 2048
HEAD_DIM = 128
SSM_CH = 768
SSM_GROUP = 16
SSM_GROUPS = SSM_CH // SSM_GROUP
SSM_STATE = 64
SWA_Q_HEADS = 4
SWA_KV_HEADS = 2
SWA_HALF = 128
DIL_PAIRS = ((128, 1), (512, 4), (2048, 16))
DIL_HEADS_PER_GROUP = 2
DIL_HEADS = DIL_HEADS_PER_GROUP * len(DIL_PAIRS)
ROPE_THETA = 500000.0
ROPE_DIM = HEAD_DIM // 4
D_FF = 7040
NORM_EPS = 1e-6
NEG_INF = -1e30

A_W = SSM_CH
BQ_W = SWA_Q_HEADS * HEAD_DIM
BKV_W = SWA_KV_HEADS * HEAD_DIM
C_W = DIL_HEADS * HEAD_DIM
YC_W = DIL_HEADS_PER_GROUP * HEAD_DIM
IN_COLS = A_W + BQ_W + 2 * BKV_W + 3 * C_W
OUT_ROWS = A_W + BQ_W + YC_W
BQ_H0 = A_W // HEAD_DIM
BK_H0 = BQ_H0 + SWA_Q_HEADS
BV_H0 = BK_H0 + SWA_KV_HEADS
CQ_H0 = BV_H0 + SWA_KV_HEADS
CK_H0 = CQ_H0 + DIL_HEADS
CV_H0 = CK_H0 + DIL_HEADS

LANES = 128
SUBLANES = 8
BF16_ROWS = 16
MXU_DIM = 256
VMEM_LIMIT = 56 << 20

TOKEN_TILE = 512
S5_CHUNK = SUBLANES
S5_BLOCK_GROUPS = LANES // SSM_GROUP
S5_BLOCKS = SSM_GROUPS // S5_BLOCK_GROUPS
S5_BLOCK_W = S5_CHUNK * LANES
S5_STATE_W = S5_BLOCK_GROUPS * SSM_STATE
S5_ROWS = 512
ATTN_UNROLL = 4
D_FF_PAD = -(-D_FF // (2 * MXU_DIM)) * (2 * MXU_DIM)


def _params(sem):
    return pltpu.CompilerParams(dimension_semantics=sem, vmem_limit_bytes=VMEM_LIMIT)


def _rms(x, g):
    ms = jnp.mean(x * x, axis=-1, keepdims=True)
    return x * lax.rsqrt(ms + NORM_EPS) * g


def _gelu_tanh(x):
    c = math.sqrt(2.0 / math.pi)
    return 0.5 * x * (1.0 + jnp.tanh(c * (x + 0.044715 * (x * x * x))))


def _batch_tile(bsz):
    return min(bsz, SUBLANES)


def _is_rope_head(h):
    return BQ_H0 <= h < BV_H0 or CQ_H0 <= h < CV_H0


def _inproj_kernel(x_ref, g_ref, w_ref, cos_ref, sa_ref, sb_ref, o_ref, u_ref, xn_ref, piece_ref,
                   *, bsz, tq, tn):
    rows = bsz * tq
    q = S5_CHUNK
    chunks = tq // q
    heads_per_tile = tn // HEAD_DIM
    u_heads = A_W // HEAD_DIM
    xn_ref[...] = _rms(x_ref[...].reshape(rows, D_MODEL), g_ref[...]).astype(BF16)

    def per_seq(v):
        return v.reshape(bsz, tq, HEAD_DIM)

    for jt in range(IN_COLS // tn):
        acc = jnp.dot(xn_ref[...], w_ref[:, jt * tn:(jt + 1) * tn], preferred_element_type=F32)
        for h in range(heads_per_tile):
            head = jt * heads_per_tile + h
            cols = slice(head * HEAD_DIM, (head + 1) * HEAD_DIM)
            zh = acc[:, h * HEAD_DIM:(h + 1) * HEAD_DIM]
            if _is_rope_head(head):
                r = (per_seq(zh) * cos_ref[...][None]
                     + per_seq(pltpu.roll(zh, HEAD_DIM - ROPE_DIM // 2, 1)) * sa_ref[...][None]
                     + per_seq(pltpu.roll(zh, ROPE_DIM // 2, 1)) * sb_ref[...][None])
                o_ref[:, :, cols] = r.astype(BF16)
            else:
                o_ref[:, :, cols] = per_seq(zh).astype(BF16)
            if head < u_heads:
                piece_ref[head] = zh
                for b in range(bsz):
                    for t in range(q):
                        u_ref[b, :, head * S5_BLOCK_W + t * LANES:head * S5_BLOCK_W + (t + 1) * LANES] = (
                            piece_ref[head, pl.ds(b * tq + t, chunks, stride=q), :])


def _inproj(x3, g_all, w_all, layer, tabs, *, tn=512):
    bsz, seq_len, _ = x3.shape
    bt = _batch_tile(bsz)
    tq = TOKEN_TILE // bt
    chunks = tq // S5_CHUNK
    cos_t, sa_t, sb_t = tabs
    tab_spec = pl.BlockSpec((tq, HEAD_DIM), lambda b, i: (i, 0))
    return pl.pallas_call(
        functools.partial(_inproj_kernel, bsz=bt, tq=tq, tn=tn),
        grid=(bsz // bt, seq_len // tq),
        in_specs=[
            pl.BlockSpec((bt, tq, D_MODEL), lambda b, i: (b, i, 0)),
            pl.BlockSpec((None, 1, D_MODEL), lambda b, i: (layer, 0, 0)),
            pl.BlockSpec((None, D_MODEL, IN_COLS), lambda b, i: (layer, 0, 0),
                         pipeline_mode=pl.Buffered(1)),
            tab_spec, tab_spec, tab_spec,
        ],
        out_specs=[
            pl.BlockSpec((bt, tq, IN_COLS), lambda b, i: (b, i, 0)),
            pl.BlockSpec((bt, chunks, S5_BLOCKS * S5_BLOCK_W), lambda b, i: (b, i, 0)),
        ],
        out_shape=[
            jax.ShapeDtypeStruct((bsz, seq_len, IN_COLS), BF16),
            jax.ShapeDtypeStruct((bsz, seq_len // S5_CHUNK, S5_BLOCKS * S5_BLOCK_W), F32),
        ],
        scratch_shapes=[pltpu.VMEM((bt * tq, D_MODEL), BF16),
                        pltpu.VMEM((A_W // HEAD_DIM, bt * tq, LANES), F32)],
        compiler_params=_params(("parallel", "parallel")),
        name="inproj",
    )(x3, g_all, w_all, cos_t, sa_t, sb_t)


def _rope_tables(seq_len):
    half = ROPE_DIM // 2
    inv = ROPE_THETA ** (-jnp.arange(0, ROPE_DIM, 2, dtype=F32) / ROPE_DIM)
    ang = jnp.arange(seq_len, dtype=F32)[:, None] * inv[None, :]
    cos, sin = jnp.cos(ang), jnp.sin(ang)
    zeros = jnp.zeros((seq_len, HEAD_DIM - ROPE_DIM), F32)
    zh = jnp.zeros((seq_len, half), F32)
    cos_t = jnp.concatenate([cos, cos, jnp.ones((seq_len, HEAD_DIM - ROPE_DIM), F32)], axis=1)
    sa_t = jnp.concatenate([-sin, zh, zeros], axis=1)
    sb_t = jnp.concatenate([zh, sin, zeros], axis=1)
    return cos_t, sa_t, sb_t


def _s5prep_kernel(lr_ref, li_ref, ldt_ref, bre_ref, bim_ref, cre_ref, cim_ref, d_ref,
                   toep_ref, w1f_ref, w1b_ref, w2f_ref, w2b_ref, aq_ref, wt_scr, ct_scr, kt_scr):
    q, p, c, gb = S5_CHUNK, SSM_STATE, SSM_GROUP, S5_BLOCK_GROUPS
    sw = S5_STATE_W
    c_bits, p_bits = c.bit_length() - 1, p.bit_length() - 1
    nt = (((1,), (1,)), ((), ()))
    rows16 = lax.broadcasted_iota(jnp.int32, (c, c), 0)
    cols16 = lax.broadcasted_iota(jnp.int32, (c, c), 1)
    for gl in range(gb):
        grp = slice(gl * c, (gl + 1) * c)
        kts = []
        for r in range(2):
            lr = lr_ref[r, gl]
            li = li_ref[r, gl]
            dt = jnp.exp(ldt_ref[r, gl])
            mag = jnp.exp(lr * dt)
            ab_re = mag * jnp.cos(li * dt)
            ab_im = mag * jnp.sin(li * dt)
            nr = ab_re - 1.0
            den = lr * lr + li * li
            z_re = (nr * lr + ab_im * li) / den
            z_im = (ab_im * lr - nr * li) / den
            b_re = bre_ref[r, gl]
            b_im = bim_ref[r, gl]
            bb_re = z_re * b_re - z_im * b_im
            bb_im = z_re * b_im + z_im * b_re
            c_re = cre_ref[r, gl]
            c_im = cim_ref[r, gl]
            pw_re = jnp.ones((1, p), F32)
            pw_im = jnp.zeros((1, p), F32)
            for k in range(q + 1):
                if k < q:
                    wt_scr[r, k, grp, 0:p] = pw_re * bb_re - pw_im * bb_im
                    wt_scr[r, k, grp, p:2 * p] = pw_re * bb_im + pw_im * bb_re
                ct_scr[r, k, grp, 0:p] = c_re * pw_re - c_im * pw_im
                ct_scr[r, k, grp, p:2 * p] = -(c_re * pw_im + c_im * pw_re)
                if k < q:
                    pw_re, pw_im = pw_re * ab_re - pw_im * ab_im, pw_re * ab_im + pw_im * ab_re
            aq_ref[r, 0:1, gl * p:(gl + 1) * p] = pw_re
            aq_ref[r, 1:2, gl * p:(gl + 1) * p] = pw_im
            wt_flat = jnp.concatenate([wt_scr[r, k, grp, :] for k in range(q)], axis=0)
            kts.append(lax.dot_general(wt_flat, ct_scr[r, 0, grp, :], nt,
                                       precision=lax.Precision.HIGHEST,
                                       preferred_element_type=F32))
        ktf, ktb = kts
        diag = jnp.where(rows16 == cols16, jnp.broadcast_to(d_ref[gl], (c, c)), 0.0)
        kt_scr[q - 1, grp, :] = ktf[0:c] + ktb[0:c] + diag
        for k in range(1, q):
            blk = slice(k * c, (k + 1) * c)
            kt_scr[q - 1 + k, grp, :] = ktf[blk]
            kt_scr[q - 1 - k, grp, :] = ktb[blk]

    def iota(shape, axis):
        return lax.broadcasted_iota(jnp.int32, shape, axis)

    def spread(x, sel):
        return jnp.dot(x, sel, precision=lax.Precision.HIGHEST, preferred_element_type=F32)

    in_mask = iota((LANES, sw), 0) >> c_bits == iota((LANES, sw), 1) >> p_bits
    in_sel = [(iota((LANES, sw), 0) == ri * p + (iota((LANES, sw), 1) & (p - 1))).astype(F32)
              for ri in range(2)]
    for j in range(q):
        for ref, src in ((w1f_ref, wt_scr[0, q - 1 - j]), (w1b_ref, wt_scr[1, j])):
            for ri in range(2):
                ref[j * LANES:(j + 1) * LANES, ri * sw:(ri + 1) * sw] = (
                    jnp.where(in_mask, spread(src, in_sel[ri]), 0.0).astype(BF16))

    out_mask = iota((sw, LANES), 0) >> p_bits == iota((sw, LANES), 1) >> c_bits
    for t in range(q):
        for ref, src in ((w2f_ref, ct_scr[0, t + 1]), (w2b_ref, ct_scr[1, q - t])):
            src_t = src.T
            for ri in range(2):
                tiled = jnp.concatenate([src_t[ri * p:(ri + 1) * p, :]] * gb, axis=0)
                ref[ri * sw:(ri + 1) * sw, t * LANES:(t + 1) * LANES] = (
                    jnp.where(out_mask, tiled, 0.0).astype(BF16))

    lag_mask = iota((LANES, LANES), 0) >> c_bits == iota((LANES, LANES), 1) >> c_bits
    lag_sel = (iota((c, LANES), 0) == (iota((c, LANES), 1) & (c - 1))).astype(F32)
    lags = [jnp.where(lag_mask, spread(kt_scr[k], lag_sel), 0.0).astype(BF16) for k in range(2 * q - 1)]
    for j in range(q):
        for t in range(q):
            toep_ref[j * LANES:(j + 1) * LANES, t * LANES:(t + 1) * LANES] = lags[t - j + q - 1]


def _s5_prep(lam_re, lam_im, log_dt, b_re, b_im, c_re, c_im, d_skip):
    depth = lam_re.shape[0]
    p, c, q, gb = SSM_STATE, SSM_GROUP, S5_CHUNK, S5_BLOCK_GROUPS
    g = depth * SSM_GROUPS
    nb = g // gb
    bw, sw2 = S5_BLOCK_W, 2 * S5_STATE_W

    def flat(x):
        return jnp.swapaxes(x, 0, 1).reshape((2, g) + x.shape[3:])

    lr = flat(lam_re).reshape(2, g, 1, p)
    li = flat(lam_im).reshape(2, g, 1, p)
    ldt = jnp.broadcast_to(flat(log_dt).reshape(2, g, 1, 1), (2, g, 1, p))
    bt_re = flat(jnp.swapaxes(b_re, -1, -2))
    bt_im = flat(jnp.swapaxes(b_im, -1, -2))
    d3 = d_skip.reshape(g, 1, c)
    vec_spec = pl.BlockSpec((2, gb, 1, p), lambda i: (0, i, 0, 0))
    mat_spec = pl.BlockSpec((2, gb, c, p), lambda i: (0, i, 0, 0))

    def op_spec(r, w):
        return pl.BlockSpec((None, r, w), lambda i: (i, 0, 0))

    def op_shape(r, w):
        return jax.ShapeDtypeStruct((nb, r, w), BF16)

    return pl.pallas_call(
        _s5prep_kernel,
        grid=(nb,),
        in_specs=[vec_spec, vec_spec, vec_spec, mat_spec, mat_spec, mat_spec, mat_spec,
                  pl.BlockSpec((gb, 1, c), lambda i: (i, 0, 0))],
        out_specs=[op_spec(bw, bw), op_spec(bw, sw2), op_spec(bw, sw2), op_spec(sw2, bw),
                   op_spec(sw2, bw), pl.BlockSpec((None, 2, 2, S5_STATE_W), lambda i: (i, 0, 0, 0))],
        out_shape=[op_shape(bw, bw), op_shape(bw, sw2), op_shape(bw, sw2), op_shape(sw2, bw),
                   op_shape(sw2, bw), jax.ShapeDtypeStruct((nb, 2, 2, S5_STATE_W), F32)],
        scratch_shapes=[pltpu.VMEM((2, q, LANES, 2 * p), F32),
                        pltpu.VMEM((2, q + 1, LANES, 2 * p), F32),
                        pltpu.VMEM((2 * q - 1, LANES, c), F32)],
        compiler_params=_params(("parallel",)),
        name="s5prep",
    )(lr, li, ldt, bt_re, bt_im, flat(c_re), flat(c_im), d3)


def _state_slabs():
    return 2 * S5_STATE_W // LANES


def _store_states(e_ref, e, *, cb, bsz, pitch):
    for b in range(bsz):
        for k in range(_state_slabs()):
            e_ref[k, b * pitch:b * pitch + cb, :] = e[b * cb:(b + 1) * cb, k * LANES:(k + 1) * LANES]


def _load_states(e_ref, b, *, cb, pitch):
    return jnp.concatenate([e_ref[k, b * pitch:b * pitch + cb, :] for k in range(_state_slabs())], axis=1)


def _chunk_scan(e_ref, s_ref, aq_ref, *, cb, bsz, pitch, reverse):
    nl = _state_slabs()
    half = nl // 2
    a_re = [jnp.broadcast_to(aq_ref[0:1, k * LANES:(k + 1) * LANES], (bsz, LANES)) for k in range(half)]
    a_im = [jnp.broadcast_to(aq_ref[1:2, k * LANES:(k + 1) * LANES], (bsz, LANES)) for k in range(half)]

    def body(i, carry):
        c = (cb - 1 - i) if reverse else i
        rows = pl.ds(c, bsz, stride=pitch)
        e = [e_ref[k, rows, :] for k in range(nl)]
        for k in range(nl):
            e_ref[k, rows, :] = carry[k]
        re = [a_re[k] * carry[k] - a_im[k] * carry[half + k] + e[k] for k in range(half)]
        im = [a_re[k] * carry[half + k] + a_im[k] * carry[k] + e[half + k] for k in range(half)]
        return tuple(re + im)

    out = lax.fori_loop(0, cb, body, tuple(s_ref[k] for k in range(nl)))
    for k in range(nl):
        s_ref[k] = out[k]


def _s5_bwd_kernel(u_ref, w1_ref, aq_ref, r_ref, e_ref, s_ref, *, cb, bsz, pitch):
    @pl.when(pl.program_id(1) == 0)
    def _():
        s_ref[...] = jnp.zeros_like(s_ref)

    ub = u_ref[...].reshape(bsz * cb, S5_BLOCK_W).astype(BF16)
    _store_states(e_ref, jnp.dot(ub, w1_ref[...], preferred_element_type=F32), cb=cb, bsz=bsz, pitch=pitch)
    _chunk_scan(e_ref, s_ref, aq_ref, cb=cb, bsz=bsz, pitch=pitch, reverse=True)
    for b in range(bsz):
        r_ref[b] = _load_states(e_ref, b, cb=cb, pitch=pitch).astype(BF16)


def _s5_fwd_kernel(u_ref, rin_ref, t_ref, w1_ref, w2f_ref, w2b_ref, aq_ref, y_ref, e_ref, s_ref,
                   *, cb, bsz, pitch):
    @pl.when(pl.program_id(1) == 0)
    def _():
        s_ref[...] = jnp.zeros_like(s_ref)

    ub = u_ref[...].reshape(bsz * cb, S5_BLOCK_W).astype(BF16)
    _store_states(e_ref, jnp.dot(ub, w1_ref[...], preferred_element_type=F32), cb=cb, bsz=bsz, pitch=pitch)
    _chunk_scan(e_ref, s_ref, aq_ref, cb=cb, bsz=bsz, pitch=pitch, reverse=False)
    s_in = jnp.concatenate([_load_states(e_ref, b, cb=cb, pitch=pitch) for b in range(bsz)], axis=0)
    y = jnp.dot(ub, t_ref[...], preferred_element_type=F32)
    y += jnp.dot(s_in.astype(BF16), w2f_ref[...], preferred_element_type=F32)
    y += jnp.dot(rin_ref[...].reshape(bsz * cb, 2 * S5_STATE_W), w2b_ref[...], preferred_element_type=F32)
    y = _gelu_tanh(y)
    for b in range(bsz):
        for t in range(S5_CHUNK):
            y_ref[b, pl.ds(t, cb, stride=S5_CHUNK), :] = y[b * cb:(b + 1) * cb, t * LANES:(t + 1) * LANES]


def _s5_mix(uc, ops, layer):
    toep, w1f, w1b, w2f, w2b, aqb = ops
    bsz, nc, _ = uc.shape
    cb = S5_ROWS // bsz
    nblk = nc // cb
    pitch = cb + SUBLANES
    bw, sw2 = S5_BLOCK_W, 2 * S5_STATE_W
    op0 = layer * S5_BLOCKS
    nl = _state_slabs()

    def wspec(r, c):
        return pl.BlockSpec((None, r, c), lambda g, s: (op0 + g, 0, 0))

    scratch = [pltpu.VMEM((nl, bsz * pitch, LANES), F32), pltpu.VMEM((nl, bsz, LANES), F32)]
    rin = pl.pallas_call(
        functools.partial(_s5_bwd_kernel, cb=cb, bsz=bsz, pitch=pitch),
        grid=(S5_BLOCKS, nblk),
        in_specs=[pl.BlockSpec((bsz, cb, bw), lambda g, s: (0, nblk - 1 - s, g)),
                  wspec(bw, sw2),
                  pl.BlockSpec((None, None, 2, S5_STATE_W), lambda g, s: (op0 + g, 1, 0, 0))],
        out_specs=pl.BlockSpec((bsz, cb, sw2), lambda g, s: (0, nblk - 1 - s, g)),
        out_shape=jax.ShapeDtypeStruct((bsz, nc, S5_BLOCKS * sw2), BF16),
        scratch_shapes=scratch,
        compiler_params=_params(("parallel", "arbitrary")),
        name="s5bwd",
    )(uc, w1b, aqb)

    return pl.pallas_call(
        functools.partial(_s5_fwd_kernel, cb=cb, bsz=bsz, pitch=pitch),
        grid=(S5_BLOCKS, nblk),
        in_specs=[pl.BlockSpec((bsz, cb, bw), lambda g, s: (0, s, g)),
                  pl.BlockSpec((bsz, cb, sw2), lambda g, s: (0, s, g)),
                  wspec(bw, bw), wspec(bw, sw2), wspec(sw2, bw), wspec(sw2, bw),
                  pl.BlockSpec((None, None, 2, S5_STATE_W), lambda g, s: (op0 + g, 0, 0, 0))],
        out_specs=pl.BlockSpec((bsz, cb * S5_CHUNK, LANES), lambda g, s: (0, s, g)),
        out_shape=jax.ShapeDtypeStruct((bsz, nc * S5_CHUNK, SSM_CH), F32),
        scratch_shapes=scratch,
        compiler_params=_params(("parallel", "arbitrary")),
        name="s5fwd",
    )(uc, rin, toep, w1f, w2f, w2b, aqb)


def _attend(q, k, v, mask, sink):
    s = lax.dot_general(q, k, (((1,), (1,)), ((), ())), preferred_element_type=F32)
    s = jnp.where(mask, s * (HEAD_DIM ** -0.5), NEG_INF)
    mx = jnp.max(s, axis=-1, keepdims=True)
    if sink is not None:
        mx = jnp.maximum(mx, sink)
    p = jnp.exp(s - mx)
    den = jnp.sum(p, axis=-1, keepdims=True)
    if sink is not None:
        den = den + jnp.exp(sink - mx)
    o = jnp.dot(p.astype(BF16), v, preferred_element_type=F32) / den
    return o, mx + jnp.log(den)


def _band_window(n, tq, half):
    w = min(tq + 2 * half, n)
    align = math.gcd(math.gcd(tq, half), n - w) if n > w else tq
    return w, align


def _swa_kernel(sink_ref, q_ref, k_ref, v_ref, o_ref, *, n, tq, heads, layer):
    half = SWA_HALF
    w, align = _band_window(n, tq, half)
    col = lax.broadcasted_iota(jnp.int32, (heads * tq, w), 1)
    row = lax.broadcasted_iota(jnp.int32, (heads * tq, w), 0) & (tq - 1)
    head_of_row = lax.broadcasted_iota(jnp.int32, (heads * tq, 1), 0) >> (tq.bit_length() - 1)
    sink = jnp.zeros((heads * tq, 1), F32)
    for g in range(heads):
        sink = jnp.where(head_of_row == g, sink_ref[layer, pl.program_id(1) * heads + g], sink)

    def body(i, _):
        q0 = pl.multiple_of(i * tq, tq)
        start = pl.multiple_of(jnp.clip(q0 - half, 0, n - w), align)
        k = k_ref[pl.ds(start, w), :]
        v = v_ref[pl.ds(start, w), :]
        mask = jnp.abs(col - row + (start - q0)) <= half
        q = jnp.concatenate([q_ref[pl.ds(q0, tq), g * HEAD_DIM:(g + 1) * HEAD_DIM]
                             for g in range(heads)], axis=0)
        o, _ = _attend(q, k, v, mask, sink)
        for g in range(heads):
            o_ref[pl.ds(q0, tq), g * HEAD_DIM:(g + 1) * HEAD_DIM] = (
                o[g * tq:(g + 1) * tq].astype(o_ref.dtype))
        return 0

    lax.fori_loop(0, n // tq, body, 0, unroll=ATTN_UNROLL)


def _swa(z3, sink, layer, *, tq=128):
    bsz, n, _ = z3.shape
    g = SWA_Q_HEADS // SWA_KV_HEADS
    return pl.pallas_call(
        functools.partial(_swa_kernel, n=n, tq=tq, heads=g, layer=layer),
        grid=(bsz, SWA_KV_HEADS),
        in_specs=[
            pl.BlockSpec(memory_space=pltpu.SMEM),
            pl.BlockSpec((None, n, g * HEAD_DIM), lambda b, h: (b, 0, BQ_H0 // g + h)),
            pl.BlockSpec((None, n, HEAD_DIM), lambda b, h: (b, 0, BK_H0 + h)),
            pl.BlockSpec((None, n, HEAD_DIM), lambda b, h: (b, 0, BV_H0 + h)),
        ],
        out_specs=pl.BlockSpec((None, n, g * HEAD_DIM), lambda b, h: (b, 0, h)),
        out_shape=jax.ShapeDtypeStruct((bsz, n, BQ_W), BF16),
        compiler_params=_params(("parallel", "parallel")),
        name="swa",
    )(sink, z3, z3, z3)


def _dilated_kernel(*refs, seq_len, tq):
    ng = len(DIL_PAIRS)
    qkv = [refs[3 * g:3 * g + 3] for g in range(ng)]
    o_ref = refs[3 * ng]
    stage_ref, qf_ref, kf_ref, vf_ref, uo_ref, ul_ref = refs[3 * ng + 1:]

    for gi, (wdw, d) in enumerate(DIL_PAIRS):
        half = wdw // (2 * d)
        n = seq_len // d
        w, align = _band_window(n, tq, half)
        if d > 1:
            for src, dst in zip(qkv[gi], (qf_ref, kf_ref, vf_ref)):
                stage_ref[...] = src[...].astype(F32)
                for r in range(d):
                    dst[r * n:(r + 1) * n, :] = stage_ref[pl.ds(r, n, stride=d), :].astype(BF16)
            q_ref, k_ref, v_ref = qf_ref, kf_ref, vf_ref
        else:
            q_ref, k_ref, v_ref = qkv[gi]
        col = lax.broadcasted_iota(jnp.int32, (tq, w), 1)
        row = lax.broadcasted_iota(jnp.int32, (tq, w), 0)
        per_res = n // tq

        def body(blk, _, d=d, n=n, w=w, align=align, half=half, per_res=per_res, gi=gi,
                 q_ref=q_ref, k_ref=k_ref, v_ref=v_ref, col=col, row=row):
            r = blk // per_res
            q0 = pl.multiple_of((blk % per_res) * tq, tq)
            base = pl.multiple_of(r * n, tq)
            start = pl.multiple_of(jnp.clip(q0 - half, 0, n - w), align)
            k = k_ref[pl.ds(base + start, w), :]
            v = v_ref[pl.ds(base + start, w), :]
            mask = jnp.abs(col - row + (start - q0)) <= half
            o, lse = _attend(q_ref[pl.ds(base + q0, tq), :], k, v, mask, None)
            lse = jnp.broadcast_to(lse, (tq, HEAD_DIM))
            if d > 1:
                rows = pl.ds(q0 * d + r, tq, stride=d)
            else:
                rows = pl.ds(q0, tq)
            uo_ref[gi, rows, :] = o
            ul_ref[gi, rows, :] = lse
            return 0

        lax.fori_loop(0, seq_len // tq, body, 0, unroll=ATTN_UNROLL)

    def merge(i, _):
        rows = pl.ds(pl.multiple_of(i * tq, tq), tq)
        ls = [ul_ref[g, rows, :] for g in range(ng)]
        mx = functools.reduce(jnp.maximum, ls)
        es = [jnp.exp(l - mx) for l in ls]
        num = sum(e * uo_ref[g, rows, :] for g, e in enumerate(es))
        o_ref[rows, :] = (num / sum(es)).astype(o_ref.dtype)
        return 0

    lax.fori_loop(0, seq_len // tq, merge, 0)


def _dilated(z3, *, tq=128):
    bsz, seq_len, _ = z3.shape
    hp = DIL_HEADS_PER_GROUP
    ng = len(DIL_PAIRS)

    def head_spec(h0, gi):
        return pl.BlockSpec((None, seq_len, HEAD_DIM), lambda b, h: (b, 0, h0 + gi * hp + h))

    in_specs = [head_spec(h0, gi) for gi in range(ng) for h0 in (CQ_H0, CK_H0, CV_H0)]
    return pl.pallas_call(
        functools.partial(_dilated_kernel, seq_len=seq_len, tq=tq),
        grid=(bsz, hp),
        in_specs=in_specs,
        out_specs=pl.BlockSpec((None, seq_len, HEAD_DIM), lambda b, h: (b, 0, h)),
        out_shape=jax.ShapeDtypeStruct((bsz, seq_len, YC_W), BF16),
        scratch_shapes=[pltpu.VMEM((seq_len, HEAD_DIM), F32),
                        pltpu.VMEM((seq_len, HEAD_DIM), BF16),
                        pltpu.VMEM((seq_len, HEAD_DIM), BF16),
                        pltpu.VMEM((seq_len, HEAD_DIM), BF16),
                        pltpu.VMEM((ng, seq_len, HEAD_DIM), F32),
                        pltpu.VMEM((ng, seq_len, HEAD_DIM), F32)],
        compiler_params=_params(("parallel", "parallel")),
        name="dilated",
    )(*([z3] * (3 * ng)))


def _outproj_kernel(ya_ref, gw_ref, gb_ref, yb_ref, yc_ref, w_ref, g_ref, x_ref, out_ref, *, bsz, tq):
    rows = bsz * tq
    ya = ya_ref[...].reshape(rows, A_W)
    gate = jnp.dot(ya.astype(BF16), gw_ref[...], preferred_element_type=F32) + gb_ref[...]
    a = ya * (1.0 / (1.0 + jnp.exp(-gate)))
    acc = jnp.dot(a.astype(BF16), w_ref[0:A_W, :], preferred_element_type=F32)
    acc += jnp.dot(yb_ref[...].reshape(rows, BQ_W), w_ref[A_W:A_W + BQ_W, :],
                   preferred_element_type=F32)
    acc += jnp.dot(yc_ref[...].reshape(rows, YC_W), w_ref[A_W + BQ_W:OUT_ROWS, :],
                   preferred_element_type=F32)
    x = x_ref[...].reshape(rows, D_MODEL)
    out_ref[...] = (x + _rms(acc, g_ref[...])).reshape(bsz, tq, D_MODEL)


def _outproj(ya, glu_w, glu_b, yb, yc, w_out, g_post, layer, x3):
    bsz, seq_len, _ = x3.shape
    bt = _batch_tile(bsz)
    tq = TOKEN_TILE // bt

    def tile(width):
        return pl.BlockSpec((bt, tq, width), lambda b, i: (b, i, 0))

    def whole(shape):
        return pl.BlockSpec((None,) + shape, lambda b, i: (layer, 0, 0))

    return pl.pallas_call(
        functools.partial(_outproj_kernel, bsz=bt, tq=tq),
        grid=(bsz // bt, seq_len // tq),
        in_specs=[tile(A_W), whole((A_W, A_W)), whole((1, A_W)), tile(BQ_W), tile(YC_W),
                  whole((OUT_ROWS, D_MODEL)), whole((1, D_MODEL)), tile(D_MODEL)],
        out_specs=tile(D_MODEL),
        out_shape=jax.ShapeDtypeStruct((bsz, seq_len, D_MODEL), F32),
        compiler_params=_params(("parallel", "parallel")),
        name="outproj",
    )(ya, glu_w, glu_b, yb, yc, w_out, g_post, x3)


def _ffn_kernel(xp_ref, x_ref, xq_ref, gpre_ref, wg_ref, wu_ref, cw_ref, cb_ref, wd_ref, gpost_ref,
                o_ref, xn_ref, g_ref, *, tm, tiles_per_seq):
    i = pl.program_id(0)
    j = pl.program_id(1)
    hb = BF16_ROWS

    @pl.when(j == 0)
    def _():
        gpre = gpre_ref[...]
        xn_ref[0:hb, :] = _rms(xp_ref[...], gpre).astype(BF16)
        xn_ref[hb:hb + tm, :] = _rms(x_ref[...], gpre).astype(BF16)
        xn_ref[hb + tm:2 * hb + tm, :] = _rms(xq_ref[...], gpre).astype(BF16)
        o_ref[...] = jnp.zeros_like(o_ref)

    pos = i % tiles_per_seq
    row = lax.broadcasted_iota(jnp.int32, (tm, 1), 0)
    keep_prev = jnp.logical_or(row > 0, pos > 0)
    keep_next = jnp.logical_or(row < tm - 1, pos < tiles_per_seq - 1)
    g_ref[...] = jnp.dot(xn_ref[...], wg_ref[...], preferred_element_type=F32)
    up = jnp.dot(xn_ref[hb:hb + tm, :], wu_ref[...], preferred_element_type=F32)
    g_prev = jnp.where(keep_prev, g_ref[hb - 1:hb - 1 + tm, :], 0.0)
    g_next = jnp.where(keep_next, g_ref[hb + 1:hb + 1 + tm, :], 0.0)
    g = (g_prev * cw_ref[0:1, :] + g_ref[hb:hb + tm, :] * cw_ref[1:2, :] + g_next * cw_ref[2:3, :]
         + cb_ref[...])
    h = _gelu_tanh(g) * up
    o_ref[...] += jnp.dot(h.astype(BF16), wd_ref[...], preferred_element_type=F32)

    @pl.when(j == pl.num_programs(1) - 1)
    def _():
        o_ref[...] = x_ref[...] + _rms(o_ref[...], gpost_ref[...])


def _ffn(x, g_pre, wg, wu, cw, cb, wd, g_post, layer, seq_len, *, tm=TOKEN_TILE, tf=1024):
    t = x.shape[0]
    hb = BF16_ROWS
    tiles_per_seq = seq_len // tm
    halo_blocks = t // hb
    r = tm // hb
    return pl.pallas_call(
        functools.partial(_ffn_kernel, tm=tm, tiles_per_seq=tiles_per_seq),
        grid=(t // tm, D_FF_PAD // tf),
        in_specs=[
            pl.BlockSpec((hb, D_MODEL), lambda i, j: (jnp.maximum(i * r - 1, 0), 0)),
            pl.BlockSpec((tm, D_MODEL), lambda i, j: (i, 0)),
            pl.BlockSpec((hb, D_MODEL), lambda i, j: (jnp.minimum((i + 1) * r, halo_blocks - 1), 0)),
            pl.BlockSpec((None, 1, D_MODEL), lambda i, j: (layer, 0, 0)),
            pl.BlockSpec((None, D_MODEL, tf), lambda i, j: (layer, 0, j)),
            pl.BlockSpec((None, D_MODEL, tf), lambda i, j: (layer, 0, j)),
            pl.BlockSpec((None, 3, tf), lambda i, j: (layer, 0, j)),
            pl.BlockSpec((None, 1, tf), lambda i, j: (layer, 0, j)),
            pl.BlockSpec((None, tf, D_MODEL), lambda i, j: (layer, j, 0)),
            pl.BlockSpec((None, 1, D_MODEL), lambda i, j: (layer, 0, 0)),
        ],
        out_specs=pl.BlockSpec((tm, D_MODEL), lambda i, j: (i, 0)),
        out_shape=jax.ShapeDtypeStruct((t, D_MODEL), F32),
        scratch_shapes=[pltpu.VMEM((tm + 2 * hb, D_MODEL), BF16),
                        pltpu.VMEM((tm + 2 * hb, tf), F32)],
        compiler_params=_params(("parallel", "arbitrary")),
        name="ffn",
    )(x, x, x, g_pre, wg, wu, cw, cb, wd, g_post)


def _prepare(ln_mix_pre, ln_mix_post, w_in, ssm_lam_re, ssm_lam_im, ssm_log_dt, ssm_b_re, ssm_b_im,
             ssm_c_re, ssm_c_im, ssm_d, ssm_glu_w, ssm_glu_b, swa_sink, w_out, ln_ffn_pre, ln_ffn_post,
             ffn_w_gate, ffn_w_up, ffn_conv_w, ffn_conv_b, ffn_w_down):
    depth = w_in.shape[0]
    fpad = D_FF_PAD - D_FF

    def row(v):
        return v.reshape(depth, 1, v.shape[-1])

    return dict(
        ln_mix_pre=row(ln_mix_pre), ln_mix_post=row(ln_mix_post),
        w_in=w_in.astype(BF16),
        s5=_s5_prep(ssm_lam_re, ssm_lam_im, ssm_log_dt, ssm_b_re, ssm_b_im, ssm_c_re, ssm_c_im, ssm_d),
        glu_w=ssm_glu_w.astype(BF16), glu_b=row(ssm_glu_b),
        sink=swa_sink,
        w_out=w_out.astype(BF16),
        ln_ffn_pre=row(ln_ffn_pre), ln_ffn_post=row(ln_ffn_post),
        wg=jnp.pad(ffn_w_gate.astype(BF16), ((0, 0), (0, 0), (0, fpad))),
        wu=jnp.pad(ffn_w_up.astype(BF16), ((0, 0), (0, 0), (0, fpad))),
        cw=jnp.pad(ffn_conv_w, ((0, 0), (0, 0), (0, fpad))),
        cb=jnp.pad(row(ffn_conv_b), ((0, 0), (0, 0), (0, fpad))),
        wd=jnp.pad(ffn_w_down.astype(BF16), ((0, 0), (0, fpad), (0, 0))),
    )


def _layer(x3, wts, layer, tabs):
    bsz, seq_len, _ = x3.shape
    z3, uc = _inproj(x3, wts["ln_mix_pre"], wts["w_in"], layer, tabs)
    ya = _s5_mix(uc, wts["s5"], layer)
    yb = _swa(z3, wts["sink"], layer)
    yc = _dilated(z3)
    x3 = _outproj(ya, wts["glu_w"], wts["glu_b"], yb, yc, wts["w_out"], wts["ln_mix_post"], layer, x3)
    x = _ffn(x3.reshape(bsz * seq_len, D_MODEL), wts["ln_ffn_pre"], wts["wg"], wts["wu"], wts["cw"],
             wts["cb"], wts["wd"], wts["ln_ffn_post"], layer, seq_len)
    return x.reshape(bsz, seq_len, D_MODEL)


def kernel(x_prompt, x_sample, ln_mix_pre, ln_mix_post, w_in, ssm_lam_re, ssm_lam_im, ssm_log_dt, ssm_b_re, ssm_b_im, ssm_c_re, ssm_c_im, ssm_d, ssm_glu_w, ssm_glu_b, swa_sink, w_out, ln_ffn_pre, ln_ffn_post, ffn_w_gate, ffn_w_up, ffn_conv_w, ffn_conv_b, ffn_w_down):
    wts = _prepare(ln_mix_pre, ln_mix_post, w_in, ssm_lam_re, ssm_lam_im, ssm_log_dt, ssm_b_re, ssm_b_im,
                   ssm_c_re, ssm_c_im, ssm_d, ssm_glu_w, ssm_glu_b, swa_sink, w_out, ln_ffn_pre,
                   ln_ffn_post, ffn_w_gate, ffn_w_up, ffn_conv_w, ffn_conv_b, ffn_w_down)
    outs = []
    for x3 in (x_prompt, x_sample):
        tabs = _rope_tables(x3.shape[1])
        for layer in range(w_in.shape[0]):
            x3 = _layer(x3, wts, layer, tabs)
        outs.append(x3)
    return tuple(outs)
```

```python
import functools
import math

import jax
import jax.numpy as jnp
from jax import lax
from jax.experimental import pallas as pl
from jax.experimental.pallas import tpu as pltpu

F32 = jnp.float32
BF16 = jnp.bfloat16

D_MODEL = 2048
HEAD_DIM = 128
SSM_CH = 768
SSM_GROUP = 16
SSM_GROUPS = SSM_CH // SSM_GROUP
SSM_STATE = 64
SWA_Q_HEADS = 4
SWA_KV_HEADS = 2
SWA_HALF = 128
DIL_PAIRS = ((128, 1), (512, 4), (2048, 16))
DIL_HEADS_PER_GROUP = 2
DIL_HEADS = DIL_HEADS_PER_GROUP * len(DIL_PAIRS)
ROPE_THETA = 500000.0
ROPE_DIM = HEAD_DIM // 4
D_FF = 7040
NORM_EPS = 1e-6
NEG_INF = -1e30

A_W = SSM_CH
BQ_W = SWA_Q_HEADS * HEAD_DIM
BKV_W = SWA_KV_HEADS * HEAD_DIM
C_W = DIL_HEADS * HEAD_DIM
YC_W = DIL_HEADS_PER_GROUP * HEAD_DIM
IN_COLS = A_W + BQ_W + 2 * BKV_W + 3 * C_W
OUT_ROWS = A_W + BQ_W + YC_W
BQ_H0 = A_W // HEAD_DIM
BK_H0 = BQ_H0 + SWA_Q_HEADS
BV_H0 = BK_H0 + SWA_KV_HEADS
CQ_H0 = BV_H0 + SWA_KV_HEADS
CK_H0 = CQ_H0 + DIL_HEADS
CV_H0 = CK_H0 + DIL_HEADS

LANES = 128
SUBLANES = 8
BF16_ROWS = 16
MXU_DIM = 256
VMEM_LIMIT = 56 << 20

TOKEN_TILE = 512
S5_CHUNK = SUBLANES
S5_BLOCK_GROUPS = LANES // SSM_GROUP
S5_BLOCKS = SSM_GROUPS // S5_BLOCK_GROUPS
S5_BLOCK_W = S5_CHUNK * LANES
S5_STATE_W = S5_BLOCK_GROUPS * SSM_STATE
S5_ROWS = 512
ATTN_UNROLL = 8
D_FF_PAD = -(-D_FF // (2 * MXU_DIM)) * (2 * MXU_DIM)


def _params(sem):
    return pltpu.CompilerParams(dimension_semantics=sem, vmem_limit_bytes=VMEM_LIMIT)


def _rms(x, g):
    ms = jnp.mean(x * x, axis=-1, keepdims=True)
    return x * lax.rsqrt(ms + NORM_EPS) * g


def _gelu_tanh(x):
    c = math.sqrt(2.0 / math.pi)
    return 0.5 * x * (1.0 + jnp.tanh(c * (x + 0.044715 * (x * x * x))))


def _batch_tile(bsz):
    return min(bsz, SUBLANES)


def _is_rope_head(h):
    return BQ_H0 <= h < BV_H0 or CQ_H0 <= h < CV_H0


def _inproj_kernel(x_ref, g_ref, w_ref, cos_ref, sa_ref, sb_ref, o_ref, u_ref, xn_ref, piece_ref,
                   *, bsz, tq, tn):
    rows = bsz * tq
    q = S5_CHUNK
    chunks = tq // q
    heads_per_tile = tn // HEAD_DIM
    u_heads = A_W // HEAD_DIM
    xn_ref[...] = _rms(x_ref[...].reshape(rows, D_MODEL), g_ref[...]).astype(BF16)

    def per_seq(v):
        return v.reshape(bsz, tq, HEAD_DIM)

    for jt in range(IN_COLS // tn):
        acc = jnp.dot(xn_ref[...], w_ref[:, jt * tn:(jt + 1) * tn], preferred_element_type=F32)
        for h in range(heads_per_tile):
            head = jt * heads_per_tile + h
            cols = slice(head * HEAD_DIM, (head + 1) * HEAD_DIM)
            zh = acc[:, h * HEAD_DIM:(h + 1) * HEAD_DIM]
            if _is_rope_head(head):
                r = (per_seq(zh) * cos_ref[...][None]
                     + per_seq(pltpu.roll(zh, HEAD_DIM - ROPE_DIM // 2, 1)) * sa_ref[...][None]
                     + per_seq(pltpu.roll(zh, ROPE_DIM // 2, 1)) * sb_ref[...][None])
                o_ref[:, :, cols] = r.astype(BF16)
            else:
                o_ref[:, :, cols] = per_seq(zh).astype(BF16)
            if head < u_heads:
                piece_ref[head] = zh
                for b in range(bsz):
                    for t in range(q):
                        u_ref[b, :, head * S5_BLOCK_W + t * LANES:head * S5_BLOCK_W + (t + 1) * LANES] = (
                            piece_ref[head, pl.ds(b * tq + t, chunks, stride=q), :])


def _inproj(x3, g_all, w_all, layer, tabs, *, tn=512):
    bsz, seq_len, _ = x3.shape
    bt = _batch_tile(bsz)
    tq = TOKEN_TILE // bt
    chunks = tq // S5_CHUNK
    cos_t, sa_t, sb_t = tabs
    tab_spec = pl.BlockSpec((tq, HEAD_DIM), lambda b, i: (i, 0))
    return pl.pallas_call(
        functools.partial(_inproj_kernel, bsz=bt, tq=tq, tn=tn),
        grid=(bsz // bt, seq_len // tq),
        in_specs=[
            pl.BlockSpec((bt, tq, D_MODEL), lambda b, i: (b, i, 0)),
            pl.BlockSpec((None, 1, D_MODEL), lambda b, i: (layer, 0, 0)),
            pl.BlockSpec((None, D_MODEL, IN_COLS), lambda b, i: (layer, 0, 0),
                         pipeline_mode=pl.Buffered(1)),
            tab_spec, tab_spec, tab_spec,
        ],
        out_specs=[
            pl.BlockSpec((bt, tq, IN_COLS), lambda b, i: (b, i, 0)),
            pl.BlockSpec((bt, chunks, S5_BLOCKS * S5_BLOCK_W), lambda b, i: (b, i, 0)),
        ],
        out_shape=[
            jax.ShapeDtypeStruct((bsz, seq_len, IN_COLS), BF16),
            jax.ShapeDtypeStruct((bsz, seq_len // S5_CHUNK, S5_BLOCKS * S5_BLOCK_W), F32),
        ],
        scratch_shapes=[pltpu.VMEM((bt * tq, D_MODEL), BF16),
                        pltpu.VMEM((A_W // HEAD_DIM, bt * tq, LANES), F32)],
        compiler_params=_params(("parallel", "parallel")),
        name="inproj",
    )(x3, g_all, w_all, cos_t, sa_t, sb_t)


def _rope_tables(seq_len):
    half = ROPE_DIM // 2
    inv = ROPE_THETA ** (-jnp.arange(0, ROPE_DIM, 2, dtype=F32) / ROPE_DIM)
    ang = jnp.arange(seq_len, dtype=F32)[:, None] * inv[None, :]
    cos, sin = jnp.cos(ang), jnp.sin(ang)
    zeros = jnp.zeros((seq_len, HEAD_DIM - ROPE_DIM), F32)
    zh = jnp.zeros((seq_len, half), F32)
    cos_t = jnp.concatenate([cos, cos, jnp.ones((seq_len, HEAD_DIM - ROPE_DIM), F32)], axis=1)
    sa_t = jnp.concatenate([-sin, zh, zeros], axis=1)
    sb_t = jnp.concatenate([zh, sin, zeros], axis=1)
    return cos_t, sa_t, sb_t


def _s5prep_kernel(lr_ref, li_ref, ldt_ref, bre_ref, bim_ref, cre_ref, cim_ref, d_ref,
                   toep_ref, w1f_ref, w1b_ref, w2f_ref, w2b_ref, aq_ref, wt_scr, ct_scr, kt_scr):
    q, p, c, gb = S5_CHUNK, SSM_STATE, SSM_GROUP, S5_BLOCK_GROUPS
    sw = S5_STATE_W
    c_bits, p_bits = c.bit_length() - 1, p.bit_length() - 1
    nt = (((1,), (1,)), ((), ()))
    rows16 = lax.broadcasted_iota(jnp.int32, (c, c), 0)
    cols16 = lax.broadcasted_iota(jnp.int32, (c, c), 1)
    for gl in range(gb):
        grp = slice(gl * c, (gl + 1) * c)
        kts = []
        for r in range(2):
            lr = lr_ref[r, gl]
            li = li_ref[r, gl]
            dt = jnp.exp(ldt_ref[r, gl])
            mag = jnp.exp(lr * dt)
            ab_re = mag * jnp.cos(li * dt)
            ab_im = mag * jnp.sin(li * dt)
            nr = ab_re - 1.0
            den = lr * lr + li * li
            z_re = (nr * lr + ab_im * li) / den
            z_im = (ab_im * lr - nr * li) / den
            b_re = bre_ref[r, gl]
            b_im = bim_ref[r, gl]
            bb_re = z_re * b_re - z_im * b_im
            bb_im = z_re * b_im + z_im * b_re
            c_re = cre_ref[r, gl]
            c_im = cim_ref[r, gl]
            pw_re = jnp.ones((1, p), F32)
            pw_im = jnp.zeros((1, p), F32)
            for k in range(q + 1):
                if k < q:
                    wt_scr[r, k, grp, 0:p] = pw_re * bb_re - pw_im * bb_im
                    wt_scr[r, k, grp, p:2 * p] = pw_re * bb_im + pw_im * bb_re
                ct_scr[r, k, grp, 0:p] = c_re * pw_re - c_im * pw_im
                ct_scr[r, k, grp, p:2 * p] = -(c_re * pw_im + c_im * pw_re)
                if k < q:
                    pw_re, pw_im = pw_re * ab_re - pw_im * ab_im, pw_re * ab_im + pw_im * ab_re
            aq_ref[r, 0:1, gl * p:(gl + 1) * p] = pw_re
            aq_ref[r, 1:2, gl * p:(gl + 1) * p] = pw_im
            wt_flat = jnp.concatenate([wt_scr[r, k, grp, :] for k in range(q)], axis=0)
            kts.append(lax.dot_general(wt_flat, ct_scr[r, 0, grp, :], nt,
                                       precision=lax.Precision.HIGHEST,
                                       preferred_element_type=F32))
        ktf, ktb = kts
        diag = jnp.where(rows16 == cols16, jnp.broadcast_to(d_ref[gl], (c, c)), 0.0)
        kt_scr[q - 1, grp, :] = ktf[0:c] + ktb[0:c] + diag
        for k in range(1, q):
            blk = slice(k * c, (k + 1) * c)
            kt_scr[q - 1 + k, grp, :] = ktf[blk]
            kt_scr[q - 1 - k, grp, :] = ktb[blk]

    def iota(shape, axis):
        return lax.broadcasted_iota(jnp.int32, shape, axis)

    def spread(x, sel):
        return jnp.dot(x, sel, precision=lax.Precision.HIGHEST, preferred_element_type=F32)

    in_mask = iota((LANES, sw), 0) >> c_bits == iota((LANES, sw), 1) >> p_bits
    in_sel = [(iota((LANES, sw), 0) == ri * p + (iota((LANES, sw), 1) & (p - 1))).astype(F32)
              for ri in range(2)]
    for j in range(q):
        for ref, src in ((w1f_ref, wt_scr[0, q - 1 - j]), (w1b_ref, wt_scr[1, j])):
            for ri in range(2):
                ref[j * LANES:(j + 1) * LANES, ri * sw:(ri + 1) * sw] = (
                    jnp.where(in_mask, spread(src, in_sel[ri]), 0.0).astype(BF16))

    out_mask = iota((sw, LANES), 0) >> p_bits == iota((sw, LANES), 1) >> c_bits
    for t in range(q):
        for ref, src in ((w2f_ref, ct_scr[0, t + 1]), (w2b_ref, ct_scr[1, q - t])):
            src_t = src.T
            for ri in range(2):
                tiled = jnp.concatenate([src_t[ri * p:(ri + 1) * p, :]] * gb, axis=0)
                ref[ri * sw:(ri + 1) * sw, t * LANES:(t + 1) * LANES] = (
                    jnp.where(out_mask, tiled, 0.0).astype(BF16))

    lag_mask = iota((LANES, LANES), 0) >> c_bits == iota((LANES, LANES), 1) >> c_bits
    lag_sel = (iota((c, LANES), 0) == (iota((c, LANES), 1) & (c - 1))).astype(F32)
    lags = [jnp.where(lag_mask, spread(kt_scr[k], lag_sel), 0.0).astype(BF16) for k in range(2 * q - 1)]
    for j in range(q):
        for t in range(q):
            toep_ref[j * LANES:(j + 1) * LANES, t * LANES:(t + 1) * LANES] = lags[t - j + q - 1]


def _s5_prep(lam_re, lam_im, log_dt, b_re, b_im, c_re, c_im, d_skip):
    depth = lam_re.shape[0]
    p, c, q, gb = SSM_STATE, SSM_GROUP, S5_CHUNK, S5_BLOCK_GROUPS
    g = depth * SSM_GROUPS
    nb = g // gb
    bw, sw2 = S5_BLOCK_W, 2 * S5_STATE_W

    def flat(x):
        return jnp.swapaxes(x, 0, 1).reshape((2, g) + x.shape[3:])

    lr = flat(lam_re).reshape(2, g, 1, p)
    li = flat(lam_im).reshape(2, g, 1, p)
    ldt = jnp.broadcast_to(flat(log_dt).reshape(2, g, 1, 1), (2, g, 1, p))
    bt_re = flat(jnp.swapaxes(b_re, -1, -2))
    bt_im = flat(jnp.swapaxes(b_im, -1, -2))
    d3 = d_skip.reshape(g, 1, c)
    vec_spec = pl.BlockSpec((2, gb, 1, p), lambda i: (0, i, 0, 0))
    mat_spec = pl.BlockSpec((2, gb, c, p), lambda i: (0, i, 0, 0))

    def op_spec(r, w):
        return pl.BlockSpec((None, r, w), lambda i: (i, 0, 0))

    def op_shape(r, w):
        return jax.ShapeDtypeStruct((nb, r, w), BF16)

    return pl.pallas_call(
        _s5prep_kernel,
        grid=(nb,),
        in_specs=[vec_spec, vec_spec, vec_spec, mat_spec, mat_spec, mat_spec, mat_spec,
                  pl.BlockSpec((gb, 1, c), lambda i: (i, 0, 0))],
        out_specs=[op_spec(bw, bw), op_spec(bw, sw2), op_spec(bw, sw2), op_spec(sw2, bw),
                   op_spec(sw2, bw), pl.BlockSpec((None, 2, 2, S5_STATE_W), lambda i: (i, 0, 0, 0))],
        out_shape=[op_shape(bw, bw), op_shape(bw, sw2), op_shape(bw, sw2), op_shape(sw2, bw),
                   op_shape(sw2, bw), jax.ShapeDtypeStruct((nb, 2, 2, S5_STATE_W), F32)],
        scratch_shapes=[pltpu.VMEM((2, q, LANES, 2 * p), F32),
                        pltpu.VMEM((2, q + 1, LANES, 2 * p), F32),
                        pltpu.VMEM((2 * q - 1, LANES, c), F32)],
        compiler_params=_params(("parallel",)),
        name="s5prep",
    )(lr, li, ldt, bt_re, bt_im, flat(c_re), flat(c_im), d3)


def _state_slabs():
    return 2 * S5_STATE_W // LANES


def _store_states(e_ref, e, *, cb, bsz, pitch):
    for b in range(bsz):
        for k in range(_state_slabs()):
            e_ref[k, b * pitch:b * pitch + cb, :] = e[b * cb:(b + 1) * cb, k * LANES:(k + 1) * LANES]


def _load_states(e_ref, b, *, cb, pitch):
    return jnp.concatenate([e_ref[k, b * pitch:b * pitch + cb, :] for k in range(_state_slabs())], axis=1)


def _chunk_scan(e_ref, s_ref, aq_ref, *, cb, bsz, pitch, reverse):
    nl = _state_slabs()
    half = nl // 2
    a_re = [jnp.broadcast_to(aq_ref[0:1, k * LANES:(k + 1) * LANES], (bsz, LANES)) for k in range(half)]
    a_im = [jnp.broadcast_to(aq_ref[1:2, k * LANES:(k + 1) * LANES], (bsz, LANES)) for k in range(half)]

    def body(i, carry):
        c = (cb - 1 - i) if reverse else i
        rows = pl.ds(c, bsz, stride=pitch)
        e = [e_ref[k, rows, :] for k in range(nl)]
        for k in range(nl):
            e_ref[k, rows, :] = carry[k]
        re = [a_re[k] * carry[k] - a_im[k] * carry[half + k] + e[k] for k in range(half)]
        im = [a_re[k] * carry[half + k] + a_im[k] * carry[k] + e[half + k] for k in range(half)]
        return tuple(re + im)

    out = lax.fori_loop(0, cb, body, tuple(s_ref[k] for k in range(nl)))
    for k in range(nl):
        s_ref[k] = out[k]


def _s5_bwd_kernel(u_ref, w1_ref, aq_ref, r_ref, e_ref, s_ref, *, cb, bsz, pitch):
    @pl.when(pl.program_id(1) == 0)
    def _():
        s_ref[...] = jnp.zeros_like(s_ref)

    ub = u_ref[...].reshape(bsz * cb, S5_BLOCK_W).astype(BF16)
    _store_states(e_ref, jnp.dot(ub, w1_ref[...], preferred_element_type=F32), cb=cb, bsz=bsz, pitch=pitch)
    _chunk_scan(e_ref, s_ref, aq_ref, cb=cb, bsz=bsz, pitch=pitch, reverse=True)
    for b in range(bsz):
        r_ref[b] = _load_states(e_ref, b, cb=cb, pitch=pitch).astype(BF16)


def _s5_fwd_kernel(u_ref, rin_ref, t_ref, w1_ref, w2f_ref, w2b_ref, aq_ref, y_ref, e_ref, s_ref,
                   *, cb, bsz, pitch):
    @pl.when(pl.program_id(1) == 0)
    def _():
        s_ref[...] = jnp.zeros_like(s_ref)

    ub = u_ref[...].reshape(bsz * cb, S5_BLOCK_W).astype(BF16)
    _store_states(e_ref, jnp.dot(ub, w1_ref[...], preferred_element_type=F32), cb=cb, bsz=bsz, pitch=pitch)
    _chunk_scan(e_ref, s_ref, aq_ref, cb=cb, bsz=bsz, pitch=pitch, reverse=False)
    s_in = jnp.concatenate([_load_states(e_ref, b, cb=cb, pitch=pitch) for b in range(bsz)], axis=0)
    y = jnp.dot(ub, t_ref[...], preferred_element_type=F32)
    y += jnp.dot(s_in.astype(BF16), w2f_ref[...], preferred_element_type=F32)
    y += jnp.dot(rin_ref[...].reshape(bsz * cb, 2 * S5_STATE_W), w2b_ref[...], preferred_element_type=F32)
    y = _gelu_tanh(y)
    for b in range(bsz):
        for t in range(S5_CHUNK):
            y_ref[b, pl.ds(t, cb, stride=S5_CHUNK), :] = y[b * cb:(b + 1) * cb, t * LANES:(t + 1) * LANES]


def _s5_mix(uc, ops, layer):
    toep, w1f, w1b, w2f, w2b, aqb = ops
    bsz, nc, _ = uc.shape
    cb = S5_ROWS // bsz
    nblk = nc // cb
    pitch = cb + SUBLANES
    bw, sw2 = S5_BLOCK_W, 2 * S5_STATE_W
    op0 = layer * S5_BLOCKS
    nl = _state_slabs()

    def wspec(r, c):
        return pl.BlockSpec((None, r, c), lambda g, s: (op0 + g, 0, 0))

    scratch = [pltpu.VMEM((nl, bsz * pitch, LANES), F32), pltpu.VMEM((nl, bsz, LANES), F32)]
    rin = pl.pallas_call(
        functools.partial(_s5_bwd_kernel, cb=cb, bsz=bsz, pitch=pitch),
        grid=(S5_BLOCKS, nblk),
        in_specs=[pl.BlockSpec((bsz, cb, bw), lambda g, s: (0, nblk - 1 - s, g)),
                  wspec(bw, sw2),
                  pl.BlockSpec((None, None, 2, S5_STATE_W), lambda g, s: (op0 + g, 1, 0, 0))],
        out_specs=pl.BlockSpec((bsz, cb, sw2), lambda g, s: (0, nblk - 1 - s, g)),
        out_shape=jax.ShapeDtypeStruct((bsz, nc, S5_BLOCKS * sw2), BF16),
        scratch_shapes=scratch,
        compiler_params=_params(("parallel", "arbitrary")),
        name="s5bwd",
    )(uc, w1b, aqb)

    return pl.pallas_call(
        functools.partial(_s5_fwd_kernel, cb=cb, bsz=bsz, pitch=pitch),
        grid=(S5_BLOCKS, nblk),
        in_specs=[pl.BlockSpec((bsz, cb, bw), lambda g, s: (0, s, g)),
                  pl.BlockSpec((bsz, cb, sw2), lambda g, s: (0, s, g)),
                  wspec(bw, bw), wspec(bw, sw2), wspec(sw2, bw), wspec(sw2, bw),
                  pl.BlockSpec((None, None, 2, S5_STATE_W), lambda g, s: (op0 + g, 0, 0, 0))],
        out_specs=pl.BlockSpec((bsz, cb * S5_CHUNK, LANES), lambda g, s: (0, s, g)),
        out_shape=jax.ShapeDtypeStruct((bsz, nc * S5_CHUNK, SSM_CH), F32),
        scratch_shapes=scratch,
        compiler_params=_params(("parallel", "arbitrary")),
        name="s5fwd",
    )(uc, rin, toep, w1f, w2f, w2b, aqb)


def _attend(q, k, v, mask, sink):
    s = lax.dot_general(q, k, (((1,), (1,)), ((), ())), preferred_element_type=F32)
    s = jnp.where(mask, s * (HEAD_DIM ** -0.5), NEG_INF)
    mx = jnp.max(s, axis=-1, keepdims=True)
    if sink is not None:
        mx = jnp.maximum(mx, sink)
    p = jnp.exp(s - mx)
    den = jnp.sum(p, axis=-1, keepdims=True)
    if sink is not None:
        den = den + jnp.exp(sink - mx)
    o = jnp.dot(p.astype(BF16), v, preferred_element_type=F32) / den
    return o, mx + jnp.log(den)


def _band_window(n, tq, half):
    w = min(tq + 2 * half, n)
    align = math.gcd(math.gcd(tq, half), n - w) if n > w else tq
    return w, align


def _swa_kernel(sink_ref, q_ref, k_ref, v_ref, o_ref, *, n, tq, heads, layer):
    half = SWA_HALF
    w, align = _band_window(n, tq, half)
    col = lax.broadcasted_iota(jnp.int32, (heads * tq, w), 1)
    row = lax.broadcasted_iota(jnp.int32, (heads * tq, w), 0) & (tq - 1)
    head_of_row = lax.broadcasted_iota(jnp.int32, (heads * tq, 1), 0) >> (tq.bit_length() - 1)
    sink = jnp.zeros((heads * tq, 1), F32)
    for g in range(heads):
        sink = jnp.where(head_of_row == g, sink_ref[layer, pl.program_id(1) * heads + g], sink)

    def body(i, _):
        q0 = pl.multiple_of(i * tq, tq)
        start = pl.multiple_of(jnp.clip(q0 - half, 0, n - w), align)
        k = k_ref[pl.ds(start, w), :]
        v = v_ref[pl.ds(start, w), :]
        mask = jnp.abs(col - row + (start - q0)) <= half
        q = jnp.concatenate([q_ref[pl.ds(q0, tq), g * HEAD_DIM:(g + 1) * HEAD_DIM]
                             for g in range(heads)], axis=0)
        o, _ = _attend(q, k, v, mask, sink)
        for g in range(heads):
            o_ref[pl.ds(q0, tq), g * HEAD_DIM:(g + 1) * HEAD_DIM] = (
                o[g * tq:(g + 1) * tq].astype(o_ref.dtype))
        return 0

    lax.fori_loop(0, n // tq, body, 0, unroll=ATTN_UNROLL)


def _swa(z3, sink, layer, *, tq=128):
    bsz, n, _ = z3.shape
    g = SWA_Q_HEADS // SWA_KV_HEADS
    return pl.pallas_call(
        functools.partial(_swa_kernel, n=n, tq=tq, heads=g, layer=layer),
        grid=(bsz, SWA_KV_HEADS),
        in_specs=[
            pl.BlockSpec(memory_space=pltpu.SMEM),
            pl.BlockSpec((None, n, g * HEAD_DIM), lambda b, h: (b, 0, BQ_H0 // g + h)),
            pl.BlockSpec((None, n, HEAD_DIM), lambda b, h: (b, 0, BK_H0 + h)),
            pl.BlockSpec((None, n, HEAD_DIM), lambda b, h: (b, 0, BV_H0 + h)),
        ],
        out_specs=pl.BlockSpec((None, n, g * HEAD_DIM), lambda b, h: (b, 0, h)),
        out_shape=jax.ShapeDtypeStruct((bsz, n, BQ_W), BF16),
        compiler_params=_params(("parallel", "parallel")),
        name="swa",
    )(sink, z3, z3, z3)


def _dilated_kernel(*refs, seq_len, tq):
    ng = len(DIL_PAIRS)
    qkv = [refs[3 * g:3 * g + 3] for g in range(ng)]
    o_ref = refs[3 * ng]
    stage_ref, qf_ref, kf_ref, vf_ref, uo_ref, ul_ref = refs[3 * ng + 1:]

    for gi, (wdw, d) in enumerate(DIL_PAIRS):
        half = wdw // (2 * d)
        n = seq_len // d
        w, align = _band_window(n, tq, half)
        if d > 1:
            for src, dst in zip(qkv[gi], (qf_ref, kf_ref, vf_ref)):
                stage_ref[...] = src[...].astype(F32)
                for r in range(d):
                    dst[r * n:(r + 1) * n, :] = stage_ref[pl.ds(r, n, stride=d), :].astype(BF16)
            q_ref, k_ref, v_ref = qf_ref, kf_ref, vf_ref
        else:
            q_ref, k_ref, v_ref = qkv[gi]
        col = lax.broadcasted_iota(jnp.int32, (tq, w), 1)
        row = lax.broadcasted_iota(jnp.int32, (tq, w), 0)
        per_res = n // tq

        def body(blk, _, d=d, n=n, w=w, align=align, half=half, per_res=per_res, gi=gi,
                 q_ref=q_ref, k_ref=k_ref, v_ref=v_ref, col=col, row=row):
            r = blk // per_res
            q0 = pl.multiple_of((blk % per_res) * tq, tq)
            base = pl.multiple_of(r * n, tq)
            start = pl.multiple_of(jnp.clip(q0 - half, 0, n - w), align)
            k = k_ref[pl.ds(base + start, w), :]
            v = v_ref[pl.ds(base + start, w), :]
            mask = jnp.abs(col - row + (start - q0)) <= half
            o, lse = _attend(q_ref[pl.ds(base + q0, tq), :], k, v, mask, None)
            lse = jnp.broadcast_to(lse, (tq, HEAD_DIM))
            if d > 1:
                rows = pl.ds(q0 * d + r, tq, stride=d)
            else:
                rows = pl.ds(q0, tq)
            uo_ref[gi, rows, :] = o
            ul_ref[gi, rows, :] = lse
            return 0

        lax.fori_loop(0, seq_len // tq, body, 0, unroll=ATTN_UNROLL)

    def merge(i, _):
        rows = pl.ds(pl.multiple_of(i * tq, tq), tq)
        ls = [ul_ref[g, rows, :] for g in range(ng)]
        mx = functools.reduce(jnp.maximum, ls)
        es = [jnp.exp(l - mx) for l in ls]
        num = sum(e * uo_ref[g, rows, :] for g, e in enumerate(es))
        o_ref[rows, :] = (num / sum(es)).astype(o_ref.dtype)
        return 0

    lax.fori_loop(0, seq_len // tq, merge, 0)


def _dilated(z3, *, tq=128):
    bsz, seq_len, _ = z3.shape
    hp = DIL_HEADS_PER_GROUP
    ng = len(DIL_PAIRS)

    def head_spec(h0, gi):
        return pl.BlockSpec((None, seq_len, HEAD_DIM), lambda b, h: (b, 0, h0 + gi * hp + h))

    in_specs = [head_spec(h0, gi) for gi in range(ng) for h0 in (CQ_H0, CK_H0, CV_H0)]
    return pl.pallas_call(
        functools.partial(_dilated_kernel, seq_len=seq_len, tq=tq),
        grid=(bsz, hp),
        in_specs=in_specs,
        out_specs=pl.BlockSpec((None, seq_len, HEAD_DIM), lambda b, h: (b, 0, h)),
        out_shape=jax.ShapeDtypeStruct((bsz, seq_len, YC_W), BF16),
        scratch_shapes=[pltpu.VMEM((seq_len, HEAD_DIM), F32),
                        pltpu.VMEM((seq_len, HEAD_DIM), BF16),
                        pltpu.VMEM((seq_len, HEAD_DIM), BF16),
                        pltpu.VMEM((seq_len, HEAD_DIM), BF16),
                        pltpu.VMEM((ng, seq_len, HEAD_DIM), F32),
                        pltpu.VMEM((ng, seq_len, HEAD_DIM), F32)],
        compiler_params=_params(("parallel", "parallel")),
        name="dilated",
    )(*([z3] * (3 * ng)))


def _outproj_kernel(ya_ref, gw_ref, gb_ref, yb_ref, yc_ref, w_ref, g_ref, x_ref, out_ref, *, bsz, tq):
    rows = bsz * tq
    ya = ya_ref[...].reshape(rows, A_W)
    gate = jnp.dot(ya.astype(BF16), gw_ref[...], preferred_element_type=F32) + gb_ref[...]
    a = ya * (1.0 / (1.0 + jnp.exp(-gate)))
    acc = jnp.dot(a.astype(BF16), w_ref[0:A_W, :], preferred_element_type=F32)
    acc += jnp.dot(yb_ref[...].reshape(rows, BQ_W), w_ref[A_W:A_W + BQ_W, :],
                   preferred_element_type=F32)
    acc += jnp.dot(yc_ref[...].reshape(rows, YC_W), w_ref[A_W + BQ_W:OUT_ROWS, :],
                   preferred_element_type=F32)
    x = x_ref[...].reshape(rows, D_MODEL)
    out_ref[...] = (x + _rms(acc, g_ref[...])).reshape(bsz, tq, D_MODEL)


def _outproj(ya, glu_w, glu_b, yb, yc, w_out, g_post, layer, x3):
    bsz, seq_len, _ = x3.shape
    bt = _batch_tile(bsz)
    tq = TOKEN_TILE // bt

    def tile(width):
        return pl.BlockSpec((bt, tq, width), lambda b, i: (b, i, 0))

    def whole(shape):
        return pl.BlockSpec((None,) + shape, lambda b, i: (layer, 0, 0))

    return pl.pallas_call(
        functools.partial(_outproj_kernel, bsz=bt, tq=tq),
        grid=(bsz // bt, seq_len // tq),
        in_specs=[tile(A_W), whole((A_W, A_W)), whole((1, A_W)), tile(BQ_W), tile(YC_W),
                  whole((OUT_ROWS, D_MODEL)), whole((1, D_MODEL)), tile(D_MODEL)],
        out_specs=tile(D_MODEL),
        out_shape=jax.ShapeDtypeStruct((bsz, seq_len, D_MODEL), F32),
        compiler_params=_params(("parallel", "parallel")),
        name="outproj",
    )(ya, glu_w, glu_b, yb, yc, w_out, g_post, x3)


def _ffn_kernel(xp_ref, x_ref, xq_ref, gpre_ref, wg_ref, wu_ref, cw_ref, cb_ref, wd_ref, gpost_ref,
                o_ref, xn_ref, g_ref, *, tm, tiles_per_seq):
    i = pl.program_id(0)
    j = pl.program_id(1)
    hb = BF16_ROWS

    @pl.when(j == 0)
    def _():
        gpre = gpre_ref[...]
        xn_ref[0:hb, :] = _rms(xp_ref[...], gpre).astype(BF16)
        xn_ref[hb:hb + tm, :] = _rms(x_ref[...], gpre).astype(BF16)
        xn_ref[hb + tm:2 * hb + tm, :] = _rms(xq_ref[...], gpre).astype(BF16)
        o_ref[...] = jnp.zeros_like(o_ref)

    pos = i % tiles_per_seq
    row = lax.broadcasted_iota(jnp.int32, (tm, 1), 0)
    keep_prev = jnp.logical_or(row > 0, pos > 0)
    keep_next = jnp.logical_or(row < tm - 1, pos < tiles_per_seq - 1)
    g_ref[...] = jnp.dot(xn_ref[...], wg_ref[...], preferred_element_type=F32)
    up = jnp.dot(xn_ref[hb:hb + tm, :], wu_ref[...], preferred_element_type=F32)
    g_prev = jnp.where(keep_prev, g_ref[hb - 1:hb - 1 + tm, :], 0.0)
    g_next = jnp.where(keep_next, g_ref[hb + 1:hb + 1 + tm, :], 0.0)
    g = (g_prev * cw_ref[0:1, :] + g_ref[hb:hb + tm, :] * cw_ref[1:2, :] + g_next * cw_ref[2:3, :]
         + cb_ref[...])
    h = _gelu_tanh(g) * up
    o_ref[...] += jnp.dot(h.astype(BF16), wd_ref[...], preferred_element_type=F32)

    @pl.when(j == pl.num_programs(1) - 1)
    def _():
        o_ref[...] = x_ref[...] + _rms(o_ref[...], gpost_ref[...])


def _ffn(x, g_pre, wg, wu, cw, cb, wd, g_post, layer, seq_len, *, tm=TOKEN_TILE, tf=1024):
    t = x.shape[0]
    hb = BF16_ROWS
    tiles_per_seq = seq_len // tm
    halo_blocks = t // hb
    r = tm // hb
    return pl.pallas_call(
        functools.partial(_ffn_kernel, tm=tm, tiles_per_seq=tiles_per_seq),
        grid=(t // tm, D_FF_PAD // tf),
        in_specs=[
            pl.BlockSpec((hb, D_MODEL), lambda i, j: (jnp.maximum(i * r - 1, 0), 0)),
            pl.BlockSpec((tm, D_MODEL), lambda i, j: (i, 0)),
            pl.BlockSpec((hb, D_MODEL), lambda i, j: (jnp.minimum((i + 1) * r, halo_blocks - 1), 0)),
            pl.BlockSpec((None, 1, D_MODEL), lambda i, j: (layer, 0, 0)),
            pl.BlockSpec((None, D_MODEL, tf), lambda i, j: (layer, 0, j)),
            pl.BlockSpec((None, D_MODEL, tf), lambda i, j: (layer, 0, j)),
            pl.BlockSpec((None, 3, tf), lambda i, j: (layer, 0, j)),
            pl.BlockSpec((None, 1, tf), lambda i, j: (layer, 0, j)),
            pl.BlockSpec((None, tf, D_MODEL), lambda i, j: (layer, j, 0)),
            pl.BlockSpec((None, 1, D_MODEL), lambda i, j: (layer, 0, 0)),
        ],
        out_specs=pl.BlockSpec((tm, D_MODEL), lambda i, j: (i, 0)),
        out_shape=jax.ShapeDtypeStruct((t, D_MODEL), F32),
        scratch_shapes=[pltpu.VMEM((tm + 2 * hb, D_MODEL), BF16),
                        pltpu.VMEM((tm + 2 * hb, tf), F32)],
        compiler_params=_params(("parallel", "arbitrary")),
        name="ffn",
    )(x, x, x, g_pre, wg, wu, cw, cb, wd, g_post)


def _prepare(ln_mix_pre, ln_mix_post, w_in, ssm_lam_re, ssm_lam_im, ssm_log_dt, ssm_b_re, ssm_b_im,
             ssm_c_re, ssm_c_im, ssm_d, ssm_glu_w, ssm_glu_b, swa_sink, w_out, ln_ffn_pre, ln_ffn_post,
             ffn_w_gate, ffn_w_up, ffn_conv_w, ffn_conv_b, ffn_w_down):
    depth = w_in.shape[0]
    fpad = D_FF_PAD - D_FF

    def row(v):
        return v.reshape(depth, 1, v.shape[-1])

    return dict(
        ln_mix_pre=row(ln_mix_pre), ln_mix_post=row(ln_mix_post),
        w_in=w_in.astype(BF16),
        s5=_s5_prep(ssm_lam_re, ssm_lam_im, ssm_log_dt, ssm_b_re, ssm_b_im, ssm_c_re, ssm_c_im, ssm_d),
        glu_w=ssm_glu_w.astype(BF16), glu_b=row(ssm_glu_b),
        sink=swa_sink,
        w_out=w_out.astype(BF16),
        ln_ffn_pre=row(ln_ffn_pre), ln_ffn_post=row(ln_ffn_post),
        wg=jnp.pad(ffn_w_gate.astype(BF16), ((0, 0), (0, 0), (0, fpad))),
        wu=jnp.pad(ffn_w_up.astype(BF16), ((0, 0), (0, 0), (0, fpad))),
        cw=jnp.pad(ffn_conv_w, ((0, 0), (0, 0), (0, fpad))),
        cb=jnp.pad(row(ffn_conv_b), ((0, 0), (0, 0), (0, fpad))),
        wd=jnp.pad(ffn_w_down.astype(BF16), ((0, 0), (0, fpad), (0, 0))),
    )


def _layer(x3, wts, layer, tabs):
    bsz, seq_len, _ = x3.shape
    z3, uc = _inproj(x3, wts["ln_mix_pre"], wts["w_in"], layer, tabs)
    ya = _s5_mix(uc, wts["s5"], layer)
    yb = _swa(z3, wts["sink"], layer)
    yc = _dilated(z3)
    x3 = _outproj(ya, wts["glu_w"], wts["glu_b"], yb, yc, wts["w_out"], wts["ln_mix_post"], layer, x3)
    x = _ffn(x3.reshape(bsz * seq_len, D_MODEL), wts["ln_ffn_pre"], wts["wg"], wts["wu"], wts["cw"],
             wts["cb"], wts["wd"], wts["ln_ffn_post"], layer, seq_len)
    return x.reshape(bsz, seq_len, D_MODEL)


def kernel(x_prompt, x_sample, ln_mix_pre, ln_mix_post, w_in, ssm_lam_re, ssm_lam_im, ssm_log_dt, ssm_b_re, ssm_b_im, ssm_c_re, ssm_c_im, ssm_d, ssm_glu_w, ssm_glu_b, swa_sink, w_out, ln_ffn_pre, ln_ffn_post, ffn_w_gate, ffn_w_up, ffn_conv_w, ffn_conv_b, ffn_w_down):
    wts = _prepare(ln_mix_pre, ln_mix_post, w_in, ssm_lam_re, ssm_lam_im, ssm_log_dt, ssm_b_re, ssm_b_im,
                   ssm_c_re, ssm_c_im, ssm_d, ssm_glu_w, ssm_glu_b, swa_sink, w_out, ln_ffn_pre,
                   ln_ffn_post, ffn_w_gate, ffn_w_up, ffn_conv_w, ffn_conv_b, ffn_w_down)
    outs = []
    for x3 in (x_prompt, x_sample):
        tabs = _rope_tables(x3.shape[1])
        for layer in range(w_in.shape[0]):
            x3 = _layer(x3, wts, layer, tabs)
        outs.append(x3)
    return tuple(outs)
```

```python
import functools
import math

import jax
import jax.numpy as jnp
from jax import lax
from jax.experimental import pallas as pl
from jax.experimental.pallas import tpu as pltpu

F32 = jnp.float32
BF16 = jnp.bfloat16

D_MODEL = 2048
HEAD_DIM = 128
SSM_CH = 768
SSM_GROUP = 16
SSM_GROUPS = SSM_CH // SSM_GROUP
SSM_STATE = 64
SWA_Q_HEADS = 4
SWA_KV_HEADS = 2
SWA_HALF = 128
DIL_PAIRS = ((128, 1), (512, 4), (2048, 16))
DIL_HEADS_PER_GROUP = 2
DIL_HEADS = DIL_HEADS_PER_GROUP * len(DIL_PAIRS)
ROPE_THETA = 500000.0
ROPE_DIM = HEAD_DIM // 4
D_FF = 7040
NORM_EPS = 1e-6
NEG_INF = -1e30

A_W = SSM_CH
BQ_W = SWA_Q_HEADS * HEAD_DIM
BKV_W = SWA_KV_HEADS * HEAD_DIM
C_W = DIL_HEADS * HEAD_DIM
YC_W = DIL_HEADS_PER_GROUP * HEAD_DIM
IN_COLS = A_W + BQ_W + 2 * BKV_W + 3 * C_W
OUT_ROWS = A_W + BQ_W + YC_W
BQ_H0 = A_W // HEAD_DIM
BK_H0 = BQ_H0 + SWA_Q_HEADS
BV_H0 = BK_H0 + SWA_KV_HEADS
CQ_H0 = BV_H0 + SWA_KV_HEADS
CK_H0 = CQ_H0 + DIL_HEADS
CV_H0 = CK_H0 + DIL_HEADS

LANES = 128
SUBLANES = 8
BF16_ROWS = 16
MXU_DIM = 256
VMEM_LIMIT = 56 << 20

TOKEN_TILE = 512
S5_CHUNK = SUBLANES
S5_BLOCK_GROUPS = LANES // SSM_GROUP
S5_BLOCKS = SSM_GROUPS // S5_BLOCK_GROUPS
S5_BLOCK_W = S5_CHUNK * LANES
S5_STATE_W = S5_BLOCK_GROUPS * SSM_STATE
S5_ROWS = 512
ATTN_UNROLL = 16
D_FF_PAD = -(-D_FF // (2 * MXU_DIM)) * (2 * MXU_DIM)


def _params(sem):
    return pltpu.CompilerParams(dimension_semantics=sem, vmem_limit_bytes=VMEM_LIMIT)


def _rms(x, g):
    ms = jnp.mean(x * x, axis=-1, keepdims=True)
    return x * lax.rsqrt(ms + NORM_EPS) * g


def _gelu_tanh(x):
    c = math.sqrt(2.0 / math.pi)
    return 0.5 * x * (1.0 + jnp.tanh(c * (x + 0.044715 * (x * x * x))))


def _batch_tile(bsz):
    return min(bsz, SUBLANES)


def _is_rope_head(h):
    return BQ_H0 <= h < BV_H0 or CQ_H0 <= h < CV_H0


def _inproj_kernel(x_ref, g_ref, w_ref, cos_ref, sa_ref, sb_ref, o_ref, u_ref, xn_ref, piece_ref,
                   *, bsz, tq, tn):
    rows = bsz * tq
    q = S5_CHUNK
    chunks = tq // q
    heads_per_tile = tn // HEAD_DIM
    u_heads = A_W // HEAD_DIM
    xn_ref[...] = _rms(x_ref[...].reshape(rows, D_MODEL), g_ref[...]).astype(BF16)

    def per_seq(v):
        return v.reshape(bsz, tq, HEAD_DIM)

    for jt in range(IN_COLS // tn):
        acc = jnp.dot(xn_ref[...], w_ref[:, jt * tn:(jt + 1) * tn], preferred_element_type=F32)
        for h in range(heads_per_tile):
            head = jt * heads_per_tile + h
            cols = slice(head * HEAD_DIM, (head + 1) * HEAD_DIM)
            zh = acc[:, h * HEAD_DIM:(h + 1) * HEAD_DIM]
            if _is_rope_head(head):
                r = (per_seq(zh) * cos_ref[...][None]
                     + per_seq(pltpu.roll(zh, HEAD_DIM - ROPE_DIM // 2, 1)) * sa_ref[...][None]
                     + per_seq(pltpu.roll(zh, ROPE_DIM // 2, 1)) * sb_ref[...][None])
                o_ref[:, :, cols] = r.astype(BF16)
            else:
                o_ref[:, :, cols] = per_seq(zh).astype(BF16)
            if head < u_heads:
                piece_ref[head] = zh
                for b in range(bsz):
                    for t in range(q):
                        u_ref[b, :, head * S5_BLOCK_W + t * LANES:head * S5_BLOCK_W + (t + 1) * LANES] = (
                            piece_ref[head, pl.ds(b * tq + t, chunks, stride=q), :])


def _inproj(x3, g_all, w_all, layer, tabs, *, tn=512):
    bsz, seq_len, _ = x3.shape
    bt = _batch_tile(bsz)
    tq = TOKEN_TILE // bt
    chunks = tq // S5_CHUNK
    cos_t, sa_t, sb_t = tabs
    tab_spec = pl.BlockSpec((tq, HEAD_DIM), lambda b, i: (i, 0))
    return pl.pallas_call(
        functools.partial(_inproj_kernel, bsz=bt, tq=tq, tn=tn),
        grid=(bsz // bt, seq_len // tq),
        in_specs=[
            pl.BlockSpec((bt, tq, D_MODEL), lambda b, i: (b, i, 0)),
            pl.BlockSpec((None, 1, D_MODEL), lambda b, i: (layer, 0, 0)),
            pl.BlockSpec((None, D_MODEL, IN_COLS), lambda b, i: (layer, 0, 0),
                         pipeline_mode=pl.Buffered(1)),
            tab_spec, tab_spec, tab_spec,
        ],
        out_specs=[
            pl.BlockSpec((bt, tq, IN_COLS), lambda b, i: (b, i, 0)),
            pl.BlockSpec((bt, chunks, S5_BLOCKS * S5_BLOCK_W), lambda b, i: (b, i, 0)),
        ],
        out_shape=[
            jax.ShapeDtypeStruct((bsz, seq_len, IN_COLS), BF16),
            jax.ShapeDtypeStruct((bsz, seq_len // S5_CHUNK, S5_BLOCKS * S5_BLOCK_W), F32),
        ],
        scratch_shapes=[pltpu.VMEM((bt * tq, D_MODEL), BF16),
                        pltpu.VMEM((A_W // HEAD_DIM, bt * tq, LANES), F32)],
        compiler_params=_params(("parallel", "parallel")),
        name="inproj",
    )(x3, g_all, w_all, cos_t, sa_t, sb_t)


def _rope_tables(seq_len):
    half = ROPE_DIM // 2
    inv = ROPE_THETA ** (-jnp.arange(0, ROPE_DIM, 2, dtype=F32) / ROPE_DIM)
    ang = jnp.arange(seq_len, dtype=F32)[:, None] * inv[None, :]
    cos, sin = jnp.cos(ang), jnp.sin(ang)
    zeros = jnp.zeros((seq_len, HEAD_DIM - ROPE_DIM), F32)
    zh = jnp.zeros((seq_len, half), F32)
    cos_t = jnp.concatenate([cos, cos, jnp.ones((seq_len, HEAD_DIM - ROPE_DIM), F32)], axis=1)
    sa_t = jnp.concatenate([-sin, zh, zeros], axis=1)
    sb_t = jnp.concatenate([zh, sin, zeros], axis=1)
    return cos_t, sa_t, sb_t


def _s5prep_kernel(lr_ref, li_ref, ldt_ref, bre_ref, bim_ref, cre_ref, cim_ref, d_ref,
                   toep_ref, w1f_ref, w1b_ref, w2f_ref, w2b_ref, aq_ref, wt_scr, ct_scr, kt_scr):
    q, p, c, gb = S5_CHUNK, SSM_STATE, SSM_GROUP, S5_BLOCK_GROUPS
    sw = S5_STATE_W
    c_bits, p_bits = c.bit_length() - 1, p.bit_length() - 1
    nt = (((1,), (1,)), ((), ()))
    rows16 = lax.broadcasted_iota(jnp.int32, (c, c), 0)
    cols16 = lax.broadcasted_iota(jnp.int32, (c, c), 1)
    for gl in range(gb):
        grp = slice(gl * c, (gl + 1) * c)
        kts = []
        for r in range(2):
            lr = lr_ref[r, gl]
            li = li_ref[r, gl]
            dt = jnp.exp(ldt_ref[r, gl])
            mag = jnp.exp(lr * dt)
            ab_re = mag * jnp.cos(li * dt)
            ab_im = mag * jnp.sin(li * dt)
            nr = ab_re - 1.0
            den = lr * lr + li * li
            z_re = (nr * lr + ab_im * li) / den
            z_im = (ab_im * lr - nr * li) / den
            b_re = bre_ref[r, gl]
            b_im = bim_ref[r, gl]
            bb_re = z_re * b_re - z_im * b_im
            bb_im = z_re * b_im + z_im * b_re
            c_re = cre_ref[r, gl]
            c_im = cim_ref[r, gl]
            pw_re = jnp.ones((1, p), F32)
            pw_im = jnp.zeros((1, p), F32)
            for k in range(q + 1):
                if k < q:
                    wt_scr[r, k, grp, 0:p] = pw_re * bb_re - pw_im * bb_im
                    wt_scr[r, k, grp, p:2 * p] = pw_re * bb_im + pw_im * bb_re
                ct_scr[r, k, grp, 0:p] = c_re * pw_re - c_im * pw_im
                ct_scr[r, k, grp, p:2 * p] = -(c_re * pw_im + c_im * pw_re)
                if k < q:
                    pw_re, pw_im = pw_re * ab_re - pw_im * ab_im, pw_re * ab_im + pw_im * ab_re
            aq_ref[r, 0:1, gl * p:(gl + 1) * p] = pw_re
            aq_ref[r, 1:2, gl * p:(gl + 1) * p] = pw_im
            wt_flat = jnp.concatenate([wt_scr[r, k, grp, :] for k in range(q)], axis=0)
            kts.append(lax.dot_general(wt_flat, ct_scr[r, 0, grp, :], nt,
                                       precision=lax.Precision.HIGHEST,
                                       preferred_element_type=F32))
        ktf, ktb = kts
        diag = jnp.where(rows16 == cols16, jnp.broadcast_to(d_ref[gl], (c, c)), 0.0)
        kt_scr[q - 1, grp, :] = ktf[0:c] + ktb[0:c] + diag
        for k in range(1, q):
            blk = slice(k * c, (k + 1) * c)
            kt_scr[q - 1 + k, grp, :] = ktf[blk]
            kt_scr[q - 1 - k, grp, :] = ktb[blk]

    def iota(shape, axis):
        return lax.broadcasted_iota(jnp.int32, shape, axis)

    def spread(x, sel):
        return jnp.dot(x, sel, precision=lax.Precision.HIGHEST, preferred_element_type=F32)

    in_mask = iota((LANES, sw), 0) >> c_bits == iota((LANES, sw), 1) >> p_bits
    in_sel = [(iota((LANES, sw), 0) == ri * p + (iota((LANES, sw), 1) & (p - 1))).astype(F32)
              for ri in range(2)]
    for j in range(q):
        for ref, src in ((w1f_ref, wt_scr[0, q - 1 - j]), (w1b_ref, wt_scr[1, j])):
            for ri in range(2):
                ref[j * LANES:(j + 1) * LANES, ri * sw:(ri + 1) * sw] = (
                    jnp.where(in_mask, spread(src, in_sel[ri]), 0.0).astype(BF16))

    out_mask = iota((sw, LANES), 0) >> p_bits == iota((sw, LANES), 1) >> c_bits
    for t in range(q):
        for ref, src in ((w2f_ref, ct_scr[0, t + 1]), (w2b_ref, ct_scr[1, q - t])):
            src_t = src.T
            for ri in range(2):
                tiled = jnp.concatenate([src_t[ri * p:(ri + 1) * p, :]] * gb, axis=0)
                ref[ri * sw:(ri + 1) * sw, t * LANES:(t + 1) * LANES] = (
                    jnp.where(out_mask, tiled, 0.0).astype(BF16))

    lag_mask = iota((LANES, LANES), 0) >> c_bits == iota((LANES, LANES), 1) >> c_bits
    lag_sel = (iota((c, LANES), 0) == (iota((c, LANES), 1) & (c - 1))).astype(F32)
    lags = [jnp.where(lag_mask, spread(kt_scr[k], lag_sel), 0.0).astype(BF16) for k in range(2 * q - 1)]
    for j in range(q):
        for t in range(q):
            toep_ref[j * LANES:(j + 1) * LANES, t * LANES:(t + 1) * LANES] = lags[t - j + q - 1]


def _s5_prep(lam_re, lam_im, log_dt, b_re, b_im, c_re, c_im, d_skip):
    depth = lam_re.shape[0]
    p, c, q, gb = SSM_STATE, SSM_GROUP, S5_CHUNK, S5_BLOCK_GROUPS
    g = depth * SSM_GROUPS
    nb = g // gb
    bw, sw2 = S5_BLOCK_W, 2 * S5_STATE_W

    def flat(x):
        return jnp.swapaxes(x, 0, 1).reshape((2, g) + x.shape[3:])

    lr = flat(lam_re).reshape(2, g, 1, p)
    li = flat(lam_im).reshape(2, g, 1, p)
    ldt = jnp.broadcast_to(flat(log_dt).reshape(2, g, 1, 1), (2, g, 1, p))
    bt_re = flat(jnp.swapaxes(b_re, -1, -2))
    bt_im = flat(jnp.swapaxes(b_im, -1, -2))
    d3 = d_skip.reshape(g, 1, c)
    vec_spec = pl.BlockSpec((2, gb, 1, p), lambda i: (0, i, 0, 0))
    mat_spec = pl.BlockSpec((2, gb, c, p), lambda i: (0, i, 0, 0))

    def op_spec(r, w):
        return pl.BlockSpec((None, r, w), lambda i: (i, 0, 0))

    def op_shape(r, w):
        return jax.ShapeDtypeStruct((nb, r, w), BF16)

    return pl.pallas_call(
        _s5prep_kernel,
        grid=(nb,),
        in_specs=[vec_spec, vec_spec, vec_spec, mat_spec, mat_spec, mat_spec, mat_spec,
                  pl.BlockSpec((gb, 1, c), lambda i: (i, 0, 0))],
        out_specs=[op_spec(bw, bw), op_spec(bw, sw2), op_spec(bw, sw2), op_spec(sw2, bw),
                   op_spec(sw2, bw), pl.BlockSpec((None, 2, 2, S5_STATE_W), lambda i: (i, 0, 0, 0))],
        out_shape=[op_shape(bw, bw), op_shape(bw, sw2), op_shape(bw, sw2), op_shape(sw2, bw),
                   op_shape(sw2, bw), jax.ShapeDtypeStruct((nb, 2, 2, S5_STATE_W), F32)],
        scratch_shapes=[pltpu.VMEM((2, q, LANES, 2 * p), F32),
                        pltpu.VMEM((2, q + 1, LANES, 2 * p), F32),
                        pltpu.VMEM((2 * q - 1, LANES, c), F32)],
        compiler_params=_params(("parallel",)),
        name="s5prep",
    )(lr, li, ldt, bt_re, bt_im, flat(c_re), flat(c_im), d3)


def _state_slabs():
    return 2 * S5_STATE_W // LANES


def _store_states(e_ref, e, *, cb, bsz, pitch):
    for b in range(bsz):
        for k in range(_state_slabs()):
            e_ref[k, b * pitch:b * pitch + cb, :] = e[b * cb:(b + 1) * cb, k * LANES:(k + 1) * LANES]


def _load_states(e_ref, b, *, cb, pitch):
    return jnp.concatenate([e_ref[k, b * pitch:b * pitch + cb, :] for k in range(_state_slabs())], axis=1)


def _chunk_scan(e_ref, s_ref, aq_ref, *, cb, bsz, pitch, reverse):
    nl = _state_slabs()
    half = nl // 2
    a_re = [jnp.broadcast_to(aq_ref[0:1, k * LANES:(k + 1) * LANES], (bsz, LANES)) for k in range(half)]
    a_im = [jnp.broadcast_to(aq_ref[1:2, k * LANES:(k + 1) * LANES], (bsz, LANES)) for k in range(half)]

    def body(i, carry):
        c = (cb - 1 - i) if reverse else i
        rows = pl.ds(c, bsz, stride=pitch)
        e = [e_ref[k, rows, :] for k in range(nl)]
        for k in range(nl):
            e_ref[k, rows, :] = carry[k]
        re = [a_re[k] * carry[k] - a_im[k] * carry[half + k] + e[k] for k in range(half)]
        im = [a_re[k] * carry[half + k] + a_im[k] * carry[k] + e[half + k] for k in range(half)]
        return tuple(re + im)

    out = lax.fori_loop(0, cb, body, tuple(s_ref[k] for k in range(nl)))
    for k in range(nl):
        s_ref[k] = out[k]


def _s5_bwd_kernel(u_ref, w1_ref, aq_ref, r_ref, e_ref, s_ref, *, cb, bsz, pitch):
    @pl.when(pl.program_id(1) == 0)
    def _():
        s_ref[...] = jnp.zeros_like(s_ref)

    ub = u_ref[...].reshape(bsz * cb, S5_BLOCK_W).astype(BF16)
    _store_states(e_ref, jnp.dot(ub, w1_ref[...], preferred_element_type=F32), cb=cb, bsz=bsz, pitch=pitch)
    _chunk_scan(e_ref, s_ref, aq_ref, cb=cb, bsz=bsz, pitch=pitch, reverse=True)
    for b in range(bsz):
        r_ref[b] = _load_states(e_ref, b, cb=cb, pitch=pitch).astype(BF16)


def _s5_fwd_kernel(u_ref, rin_ref, t_ref, w1_ref, w2f_ref, w2b_ref, aq_ref, y_ref, e_ref, s_ref,
                   *, cb, bsz, pitch):
    @pl.when(pl.program_id(1) == 0)
    def _():
        s_ref[...] = jnp.zeros_like(s_ref)

    ub = u_ref[...].reshape(bsz * cb, S5_BLOCK_W).astype(BF16)
    _store_states(e_ref, jnp.dot(ub, w1_ref[...], preferred_element_type=F32), cb=cb, bsz=bsz, pitch=pitch)
    _chunk_scan(e_ref, s_ref, aq_ref, cb=cb, bsz=bsz, pitch=pitch, reverse=False)
    s_in = jnp.concatenate([_load_states(e_ref, b, cb=cb, pitch=pitch) for b in range(bsz)], axis=0)
    y = jnp.dot(ub, t_ref[...], preferred_element_type=F32)
    y += jnp.dot(s_in.astype(BF16), w2f_ref[...], preferred_element_type=F32)
    y += jnp.dot(rin_ref[...].reshape(bsz * cb, 2 * S5_STATE_W), w2b_ref[...], preferred_element_type=F32)
    y = _gelu_tanh(y)
    for b in range(bsz):
        for t in range(S5_CHUNK):
            y_ref[b, pl.ds(t, cb, stride=S5_CHUNK), :] = y[b * cb:(b + 1) * cb, t * LANES:(t + 1) * LANES]


def _s5_mix(uc, ops, layer):
    toep, w1f, w1b, w2f, w2b, aqb = ops
    bsz, nc, _ = uc.shape
    cb = S5_ROWS // bsz
    nblk = nc // cb
    pitch = cb + SUBLANES
    bw, sw2 = S5_BLOCK_W, 2 * S5_STATE_W
    op0 = layer * S5_BLOCKS
    nl = _state_slabs()

    def wspec(r, c):
        return pl.BlockSpec((None, r, c), lambda g, s: (op0 + g, 0, 0))

    scratch = [pltpu.VMEM((nl, bsz * pitch, LANES), F32), pltpu.VMEM((nl, bsz, LANES), F32)]
    rin = pl.pallas_call(
        functools.partial(_s5_bwd_kernel, cb=cb, bsz=bsz, pitch=pitch),
        grid=(S5_BLOCKS, nblk),
        in_specs=[pl.BlockSpec((bsz, cb, bw), lambda g, s: (0, nblk - 1 - s, g)),
                  wspec(bw, sw2),
                  pl.BlockSpec((None, None, 2, S5_STATE_W), lambda g, s: (op0 + g, 1, 0, 0))],
        out_specs=pl.BlockSpec((bsz, cb, sw2), lambda g, s: (0, nblk - 1 - s, g)),
        out_shape=jax.ShapeDtypeStruct((bsz, nc, S5_BLOCKS * sw2), BF16),
        scratch_shapes=scratch,
        compiler_params=_params(("parallel", "arbitrary")),
        name="s5bwd",
    )(uc, w1b, aqb)

    return pl.pallas_call(
        functools.partial(_s5_fwd_kernel, cb=cb, bsz=bsz, pitch=pitch),
        grid=(S5_BLOCKS, nblk),
        in_specs=[pl.BlockSpec((bsz, cb, bw), lambda g, s: (0, s, g)),
                  pl.BlockSpec((bsz, cb, sw2), lambda g, s: (0, s, g)),
                  wspec(bw, bw), wspec(bw, sw2), wspec(sw2, bw), wspec(sw2, bw),
                  pl.BlockSpec((None, None, 2, S5_STATE_W), lambda g, s: (op0 + g, 0, 0, 0))],
        out_specs=pl.BlockSpec((bsz, cb * S5_CHUNK, LANES), lambda g, s: (0, s, g)),
        out_shape=jax.ShapeDtypeStruct((bsz, nc * S5_CHUNK, SSM_CH), F32),
        scratch_shapes=scratch,
        compiler_params=_params(("parallel", "arbitrary")),
        name="s5fwd",
    )(uc, rin, toep, w1f, w2f, w2b, aqb)


def _attend(q, k, v, mask, sink):
    s = lax.dot_general(q, k, (((1,), (1,)), ((), ())), preferred_element_type=F32)
    s = jnp.where(mask, s * (HEAD_DIM ** -0.5), NEG_INF)
    mx = jnp.max(s, axis=-1, keepdims=True)
    if sink is not None:
        mx = jnp.maximum(mx, sink)
    p = jnp.exp(s - mx)
    den = jnp.sum(p, axis=-1, keepdims=True)
    if sink is not None:
        den = den + jnp.exp(sink - mx)
    o = jnp.dot(p.astype(BF16), v, preferred_element_type=F32) / den
    return o, mx + jnp.log(den)


def _band_window(n, tq, half):
    w = min(tq + 2 * half, n)
    align = math.gcd(math.gcd(tq, half), n - w) if n > w else tq
    return w, align


def _swa_kernel(sink_ref, q_ref, k_ref, v_ref, o_ref, *, n, tq, heads, layer):
    half = SWA_HALF
    w, align = _band_window(n, tq, half)
    col = lax.broadcasted_iota(jnp.int32, (heads * tq, w), 1)
    row = lax.broadcasted_iota(jnp.int32, (heads * tq, w), 0) & (tq - 1)
    head_of_row = lax.broadcasted_iota(jnp.int32, (heads * tq, 1), 0) >> (tq.bit_length() - 1)
    sink = jnp.zeros((heads * tq, 1), F32)
    for g in range(heads):
        sink = jnp.where(head_of_row == g, sink_ref[layer, pl.program_id(1) * heads + g], sink)

    def body(i, _):
        q0 = pl.multiple_of(i * tq, tq)
        start = pl.multiple_of(jnp.clip(q0 - half, 0, n - w), align)
        k = k_ref[pl.ds(start, w), :]
        v = v_ref[pl.ds(start, w), :]
        mask = jnp.abs(col - row + (start - q0)) <= half
        q = jnp.concatenate([q_ref[pl.ds(q0, tq), g * HEAD_DIM:(g + 1) * HEAD_DIM]
                             for g in range(heads)], axis=0)
        o, _ = _attend(q, k, v, mask, sink)
        for g in range(heads):
            o_ref[pl.ds(q0, tq), g * HEAD_DIM:(g + 1) * HEAD_DIM] = (
                o[g * tq:(g + 1) * tq].astype(o_ref.dtype))
        return 0

    lax.fori_loop(0, n // tq, body, 0, unroll=ATTN_UNROLL)


def _swa(z3, sink, layer, *, tq=128):
    bsz, n, _ = z3.shape
    g = SWA_Q_HEADS // SWA_KV_HEADS
    return pl.pallas_call(
        functools.partial(_swa_kernel, n=n, tq=tq, heads=g, layer=layer),
        grid=(bsz, SWA_KV_HEADS),
        in_specs=[
            pl.BlockSpec(memory_space=pltpu.SMEM),
            pl.BlockSpec((None, n, g * HEAD_DIM), lambda b, h: (b, 0, BQ_H0 // g + h)),
            pl.BlockSpec((None, n, HEAD_DIM), lambda b, h: (b, 0, BK_H0 + h)),
            pl.BlockSpec((None, n, HEAD_DIM), lambda b, h: (b, 0, BV_H0 + h)),
        ],
        out_specs=pl.BlockSpec((None, n, g * HEAD_DIM), lambda b, h: (b, 0, h)),
        out_shape=jax.ShapeDtypeStruct((bsz, n, BQ_W), BF16),
        compiler_params=_params(("parallel", "parallel")),
        name="swa",
    )(sink, z3, z3, z3)


def _dilated_kernel(*refs, seq_len, tq):
    ng = len(DIL_PAIRS)
    qkv = [refs[3 * g:3 * g + 3] for g in range(ng)]
    o_ref = refs[3 * ng]
    stage_ref, qf_ref, kf_ref, vf_ref, uo_ref, ul_ref = refs[3 * ng + 1:]

    for gi, (wdw, d) in enumerate(DIL_PAIRS):
        half = wdw // (2 * d)
        n = seq_len // d
        w, align = _band_window(n, tq, half)
        if d > 1:
            for src, dst in zip(qkv[gi], (qf_ref, kf_ref, vf_ref)):
                stage_ref[...] = src[...].astype(F32)
                for r in range(d):
                    dst[r * n:(r + 1) * n, :] = stage_ref[pl.ds(r, n, stride=d), :].astype(BF16)
            q_ref, k_ref, v_ref = qf_ref, kf_ref, vf_ref
        else:
            q_ref, k_ref, v_ref = qkv[gi]
        col = lax.broadcasted_iota(jnp.int32, (tq, w), 1)
        row = lax.broadcasted_iota(jnp.int32, (tq, w), 0)
        per_res = n // tq

        def body(blk, _, d=d, n=n, w=w, align=align, half=half, per_res=per_res, gi=gi,
                 q_ref=q_ref, k_ref=k_ref, v_ref=v_ref, col=col, row=row):
            r = blk // per_res
            q0 = pl.multiple_of((blk % per_res) * tq, tq)
            base = pl.multiple_of(r * n, tq)
            start = pl.multiple_of(jnp.clip(q0 - half, 0, n - w), align)
            k = k_ref[pl.ds(base + start, w), :]
            v = v_ref[pl.ds(base + start, w), :]
            mask = jnp.abs(col - row + (start - q0)) <= half
            o, lse = _attend(q_ref[pl.ds(base + q0, tq), :], k, v, mask, None)
            lse = jnp.broadcast_to(lse, (tq, HEAD_DIM))
            if d > 1:
                rows = pl.ds(q0 * d + r, tq, stride=d)
            else:
                rows = pl.ds(q0, tq)
            uo_ref[gi, rows, :] = o
            ul_ref[gi, rows, :] = lse
            return 0

        lax.fori_loop(0, seq_len // tq, body, 0, unroll=ATTN_UNROLL)

    def merge(i, _):
        rows = pl.ds(pl.multiple_of(i * tq, tq), tq)
        ls = [ul_ref[g, rows, :] for g in range(ng)]
        mx = functools.reduce(jnp.maximum, ls)
        es = [jnp.exp(l - mx) for l in ls]
        num = sum(e * uo_ref[g, rows, :] for g, e in enumerate(es))
        o_ref[rows, :] = (num / sum(es)).astype(o_ref.dtype)
        return 0

    lax.fori_loop(0, seq_len // tq, merge, 0)


def _dilated(z3, *, tq=128):
    bsz, seq_len, _ = z3.shape
    hp = DIL_HEADS_PER_GROUP
    ng = len(DIL_PAIRS)

    def head_spec(h0, gi):
        return pl.BlockSpec((None, seq_len, HEAD_DIM), lambda b, h: (b, 0, h0 + gi * hp + h))

    in_specs = [head_spec(h0, gi) for gi in range(ng) for h0 in (CQ_H0, CK_H0, CV_H0)]
    return pl.pallas_call(
        functools.partial(_dilated_kernel, seq_len=seq_len, tq=tq),
        grid=(bsz, hp),
        in_specs=in_specs,
        out_specs=pl.BlockSpec((None, seq_len, HEAD_DIM), lambda b, h: (b, 0, h)),
        out_shape=jax.ShapeDtypeStruct((bsz, seq_len, YC_W), BF16),
        scratch_shapes=[pltpu.VMEM((seq_len, HEAD_DIM), F32),
                        pltpu.VMEM((seq_len, HEAD_DIM), BF16),
                        pltpu.VMEM((seq_len, HEAD_DIM), BF16),
                        pltpu.VMEM((seq_len, HEAD_DIM), BF16),
                        pltpu.VMEM((ng, seq_len, HEAD_DIM), F32),
                        pltpu.VMEM((ng, seq_len, HEAD_DIM), F32)],
        compiler_params=_params(("parallel", "parallel")),
        name="dilated",
    )(*([z3] * (3 * ng)))


def _outproj_kernel(ya_ref, gw_ref, gb_ref, yb_ref, yc_ref, w_ref, g_ref, x_ref, out_ref, *, bsz, tq):
    rows = bsz * tq
    ya = ya_ref[...].reshape(rows, A_W)
    gate = jnp.dot(ya.astype(BF16), gw_ref[...], preferred_element_type=F32) + gb_ref[...]
    a = ya * (1.0 / (1.0 + jnp.exp(-gate)))
    acc = jnp.dot(a.astype(BF16), w_ref[0:A_W, :], preferred_element_type=F32)
    acc += jnp.dot(yb_ref[...].reshape(rows, BQ_W), w_ref[A_W:A_W + BQ_W, :],
                   preferred_element_type=F32)
    acc += jnp.dot(yc_ref[...].reshape(rows, YC_W), w_ref[A_W + BQ_W:OUT_ROWS, :],
                   preferred_element_type=F32)
    x = x_ref[...].reshape(rows, D_MODEL)
    out_ref[...] = (x + _rms(acc, g_ref[...])).reshape(bsz, tq, D_MODEL)


def _outproj(ya, glu_w, glu_b, yb, yc, w_out, g_post, layer, x3):
    bsz, seq_len, _ = x3.shape
    bt = _batch_tile(bsz)
    tq = TOKEN_TILE // bt

    def tile(width):
        return pl.BlockSpec((bt, tq, width), lambda b, i: (b, i, 0))

    def whole(shape):
        return pl.BlockSpec((None,) + shape, lambda b, i: (layer, 0, 0))

    return pl.pallas_call(
        functools.partial(_outproj_kernel, bsz=bt, tq=tq),
        grid=(bsz // bt, seq_len // tq),
        in_specs=[tile(A_W), whole((A_W, A_W)), whole((1, A_W)), tile(BQ_W), tile(YC_W),
                  whole((OUT_ROWS, D_MODEL)), whole((1, D_MODEL)), tile(D_MODEL)],
        out_specs=tile(D_MODEL),
        out_shape=jax.ShapeDtypeStruct((bsz, seq_len, D_MODEL), F32),
        compiler_params=_params(("parallel", "parallel")),
        name="outproj",
    )(ya, glu_w, glu_b, yb, yc, w_out, g_post, x3)


def _ffn_kernel(xp_ref, x_ref, xq_ref, gpre_ref, wg_ref, wu_ref, cw_ref, cb_ref, wd_ref, gpost_ref,
                o_ref, xn_ref, g_ref, *, tm, tiles_per_seq):
    i = pl.program_id(0)
    j = pl.program_id(1)
    hb = BF16_ROWS

    @pl.when(j == 0)
    def _():
        gpre = gpre_ref[...]
        xn_ref[0:hb, :] = _rms(xp_ref[...], gpre).astype(BF16)
        xn_ref[hb:hb + tm, :] = _rms(x_ref[...], gpre).astype(BF16)
        xn_ref[hb + tm:2 * hb + tm, :] = _rms(xq_ref[...], gpre).astype(BF16)
        o_ref[...] = jnp.zeros_like(o_ref)

    pos = i % tiles_per_seq
    row = lax.broadcasted_iota(jnp.int32, (tm, 1), 0)
    keep_prev = jnp.logical_or(row > 0, pos > 0)
    keep_next = jnp.logical_or(row < tm - 1, pos < tiles_per_seq - 1)
    g_ref[...] = jnp.dot(xn_ref[...], wg_ref[...], preferred_element_type=F32)
    up = jnp.dot(xn_ref[hb:hb + tm, :], wu_ref[...], preferred_element_type=F32)
    g_prev = jnp.where(keep_prev, g_ref[hb - 1:hb - 1 + tm, :], 0.0)
    g_next = jnp.where(keep_next, g_ref[hb + 1:hb + 1 + tm, :], 0.0)
    g = (g_prev * cw_ref[0:1, :] + g_ref[hb:hb + tm, :] * cw_ref[1:2, :] + g_next * cw_ref[2:3, :]
         + cb_ref[...])
    h = _gelu_tanh(g) * up
    o_ref[...] += jnp.dot(h.astype(BF16), wd_ref[...], preferred_element_type=F32)

    @pl.when(j == pl.num_programs(1) - 1)
    def _():
        o_ref[...] = x_ref[...] + _rms(o_ref[...], gpost_ref[...])


def _ffn(x, g_pre, wg, wu, cw, cb, wd, g_post, layer, seq_len, *, tm=TOKEN_TILE, tf=1024):
    t = x.shape[0]
    hb = BF16_ROWS
    tiles_per_seq = seq_len // tm
    halo_blocks = t // hb
    r = tm // hb
    return pl.pallas_call(
        functools.partial(_ffn_kernel, tm=tm, tiles_per_seq=tiles_per_seq),
        grid=(t // tm, D_FF_PAD // tf),
        in_specs=[
            pl.BlockSpec((hb, D_MODEL), lambda i, j: (jnp.maximum(i * r - 1, 0), 0)),
            pl.BlockSpec((tm, D_MODEL), lambda i, j: (i, 0)),
            pl.BlockSpec((hb, D_MODEL), lambda i, j: (jnp.minimum((i + 1) * r, halo_blocks - 1), 0)),
            pl.BlockSpec((None, 1, D_MODEL), lambda i, j: (layer, 0, 0)),
            pl.BlockSpec((None, D_MODEL, tf), lambda i, j: (layer, 0, j)),
            pl.BlockSpec((None, D_MODEL, tf), lambda i, j: (layer, 0, j)),
            pl.BlockSpec((None, 3, tf), lambda i, j: (layer, 0, j)),
            pl.BlockSpec((None, 1, tf), lambda i, j: (layer, 0, j)),
            pl.BlockSpec((None, tf, D_MODEL), lambda i, j: (layer, j, 0)),
            pl.BlockSpec((None, 1, D_MODEL), lambda i, j: (layer, 0, 0)),
        ],
        out_specs=pl.BlockSpec((tm, D_MODEL), lambda i, j: (i, 0)),
        out_shape=jax.ShapeDtypeStruct((t, D_MODEL), F32),
        scratch_shapes=[pltpu.VMEM((tm + 2 * hb, D_MODEL), BF16),
                        pltpu.VMEM((tm + 2 * hb, tf), F32)],
        compiler_params=_params(("parallel", "arbitrary")),
        name="ffn",
    )(x, x, x, g_pre, wg, wu, cw, cb, wd, g_post)


def _prepare(ln_mix_pre, ln_mix_post, w_in, ssm_lam_re, ssm_lam_im, ssm_log_dt, ssm_b_re, ssm_b_im,
             ssm_c_re, ssm_c_im, ssm_d, ssm_glu_w, ssm_glu_b, swa_sink, w_out, ln_ffn_pre, ln_ffn_post,
             ffn_w_gate, ffn_w_up, ffn_conv_w, ffn_conv_b, ffn_w_down):
    depth = w_in.shape[0]
    fpad = D_FF_PAD - D_FF

    def row(v):
        return v.reshape(depth, 1, v.shape[-1])

    return dict(
        ln_mix_pre=row(ln_mix_pre), ln_mix_post=row(ln_mix_post),
        w_in=w_in.astype(BF16),
        s5=_s5_prep(ssm_lam_re, ssm_lam_im, ssm_log_dt, ssm_b_re, ssm_b_im, ssm_c_re, ssm_c_im, ssm_d),
        glu_w=ssm_glu_w.astype(BF16), glu_b=row(ssm_glu_b),
        sink=swa_sink,
        w_out=w_out.astype(BF16),
        ln_ffn_pre=row(ln_ffn_pre), ln_ffn_post=row(ln_ffn_post),
        wg=jnp.pad(ffn_w_gate.astype(BF16), ((0, 0), (0, 0), (0, fpad))),
        wu=jnp.pad(ffn_w_up.astype(BF16), ((0, 0), (0, 0), (0, fpad))),
        cw=jnp.pad(ffn_conv_w, ((0, 0), (0, 0), (0, fpad))),
        cb=jnp.pad(row(ffn_conv_b), ((0, 0), (0, 0), (0, fpad))),
        wd=jnp.pad(ffn_w_down.astype(BF16), ((0, 0), (0, fpad), (0, 0))),
    )


def _layer(x3, wts, layer, tabs):
    bsz, seq_len, _ = x3.shape
    z3, uc = _inproj(x3, wts["ln_mix_pre"], wts["w_in"], layer, tabs)
    ya = _s5_mix(uc, wts["s5"], layer)
    yb = _swa(z3, wts["sink"], layer)
    yc = _dilated(z3)
    x3 = _outproj(ya, wts["glu_w"], wts["glu_b"], yb, yc, wts["w_out"], wts["ln_mix_post"], layer, x3)
    x = _ffn(x3.reshape(bsz * seq_len, D_MODEL), wts["ln_ffn_pre"], wts["wg"], wts["wu"], wts["cw"],
             wts["cb"], wts["wd"], wts["ln_ffn_post"], layer, seq_len)
    return x.reshape(bsz, seq_len, D_MODEL)


def kernel(x_prompt, x_sample, ln_mix_pre, ln_mix_post, w_in, ssm_lam_re, ssm_lam_im, ssm_log_dt, ssm_b_re, ssm_b_im, ssm_c_re, ssm_c_im, ssm_d, ssm_glu_w, ssm_glu_b, swa_sink, w_out, ln_ffn_pre, ln_ffn_post, ffn_w_gate, ffn_w_up, ffn_conv_w, ffn_conv_b, ffn_w_down):
    wts = _prepare(ln_mix_pre, ln_mix_post, w_in, ssm_lam_re, ssm_lam_im, ssm_log_dt, ssm_b_re, ssm_b_im,
                   ssm_c_re, ssm_c_im, ssm_d, ssm_glu_w, ssm_glu_b, swa_sink, w_out, ln_ffn_pre,
                   ln_ffn_post, ffn_w_gate, ffn_w_up, ffn_conv_w, ffn_conv_b, ffn_w_down)
    outs = []
    for x3 in (x_prompt, x_sample):
        tabs = _rope_tables(x3.shape[1])
        for layer in range(w_in.shape[0]):
            x3 = _layer(x3, wts, layer, tabs)
        outs.append(x3)
    return tuple(outs)
```
